```python
import jax, jax.numpy as jnp
from jax import lax
import numpy as np

D_MODEL = 1024
BATCH = 8
SEQ = 2048
DEPTH = 2

D_FF = 2816
NORM_EPS = 1e-6
LRU_WIDTH = 512
LRU_HEADS = 8
LRU_HEAD_DIM = LRU_WIDTH // LRU_HEADS
LRU_CONV_WIDTH = 4
LRU_C = 8.0
MLA_HEADS = 8
QK_NOPE_DIM = 64
QK_ROPE_DIM = 32
V_HEAD_DIM = 64
Q_LORA_RANK = 384
KV_LORA_RANK = 256
ROPE_THETA = 10000.0
Q_BLOCK = 128
CONV_CH = 512
CONV_WIDTH = 31
N_BRANCH = 3
IN_A = 2 * LRU_WIDTH
IN_B = Q_LORA_RANK + KV_LORA_RANK + QK_ROPE_DIM
IN_C = 2 * CONV_CH
IN_G = N_BRANCH * D_MODEL
D_IN = IN_A + IN_B + IN_C + IN_G
MAX_POS_OFFSET = 4096

kernel_name = 'hybrid_rglru_mla_conformer_macaron'


def rms_norm(x, g):
    xf = x.astype(jnp.float32)
    y = xf * lax.rsqrt(jnp.mean(xf * xf, axis=-1, keepdims=True) + NORM_EPS)
    return (y * g.astype(jnp.float32)).astype(x.dtype)


def layer_norm(x, g, b):
    xf = x.astype(jnp.float32)
    mu = jnp.mean(xf, axis=-1, keepdims=True)
    var = jnp.mean(jnp.square(xf - mu), axis=-1, keepdims=True)
    y = (xf - mu) * lax.rsqrt(var + NORM_EPS)
    return (y * g.astype(jnp.float32) + b.astype(jnp.float32)).astype(x.dtype)


def swiglu(x, w1, w2):
    gu = x @ w1
    g, u = gu[..., :D_FF], gu[..., D_FF:]
    return (jax.nn.silu(g) * u) @ w2


def causal_depthwise_conv(x, w, b):
    k = w.shape[0]
    y = lax.conv_general_dilated(
        x, w[:, None, :].astype(x.dtype), window_strides=(1,), padding=[(k - 1, 0)],
        dimension_numbers=('NWC', 'WIO', 'NWC'), feature_group_count=x.shape[-1])
    return y + b.astype(x.dtype)


def rg_lru(x, w_gate, b_gate, lam):
    b, s, w = x.shape
    xf = x.astype(jnp.float32)
    xh = xf.reshape(b, s, LRU_HEADS, LRU_HEAD_DIM)
    gates = jnp.einsum('bshd,hde->bshe', xh, w_gate.astype(jnp.float32)) + b_gate.astype(jnp.float32)
    r = jax.nn.sigmoid(gates[..., :LRU_HEAD_DIM]).reshape(b, s, w)
    i = jax.nn.sigmoid(gates[..., LRU_HEAD_DIM:]).reshape(b, s, w)
    log_a = -LRU_C * r * jax.nn.softplus(-lam.astype(jnp.float32))
    a = jnp.exp(log_a)
    u = jnp.sqrt(-jnp.expm1(2.0 * log_a)) * (i * xf)

    def combine(left, right):
        a_l, h_l = left
        a_r, h_r = right
        return a_l * a_r, a_r * h_l + h_r

    _, h = lax.associative_scan(combine, (a, u), axis=1)
    return h.astype(x.dtype)


def rope_tables(positions):
    inv_freq = ROPE_THETA ** (-jnp.arange(0, QK_ROPE_DIM, 2, dtype=jnp.float32) / QK_ROPE_DIM)
    ang = positions.astype(jnp.float32)[..., None] * inv_freq
    return jnp.cos(ang), jnp.sin(ang)


def apply_rope(x, cos, sin):
    half = x.shape[-1] // 2
    x1, x2 = x[..., :half], x[..., half:]
    cos = cos.astype(x.dtype)
    sin = sin.astype(x.dtype)
    return jnp.concatenate([x1 * cos - x2 * sin, x2 * cos + x1 * sin], axis=-1)


def mla_branch(cq, ckv, kpe, positions, q_norm, w_uq, kv_norm, w_ukv, w_o):
    b, s, _ = cq.shape
    q = (rms_norm(cq, q_norm) @ w_uq).reshape(b, s, MLA_HEADS, QK_NOPE_DIM + QK_ROPE_DIM)
    q_nope, q_pe = q[..., :QK_NOPE_DIM], q[..., QK_NOPE_DIM:]
    kv = (rms_norm(ckv, kv_norm) @ w_ukv).reshape(b, s, MLA_HEADS, QK_NOPE_DIM + V_HEAD_DIM)
    k_nope, v = kv[..., :QK_NOPE_DIM], kv[..., QK_NOPE_DIM:]
    cos, sin = rope_tables(positions)
    q_pe = apply_rope(q_pe, cos[:, :, None, :], sin[:, :, None, :])
    k_pe = apply_rope(kpe, cos, sin)
    n_blk = s // Q_BLOCK
    qn_blocks = q_nope.reshape(b, n_blk, Q_BLOCK, MLA_HEADS, QK_NOPE_DIM).swapaxes(0, 1)
    qp_blocks = q_pe.reshape(b, n_blk, Q_BLOCK, MLA_HEADS, QK_ROPE_DIM).swapaxes(0, 1)
    scale = (QK_NOPE_DIM + QK_ROPE_DIM) ** -0.5
    key_idx = jnp.arange(s)

    def attend(args):
        qn, qp, blk = args
        sc = (jnp.einsum('bqhd,bkhd->bhqk', qn, k_nope, preferred_element_type=jnp.float32)
              + jnp.einsum('bqhr,bkr->bhqk', qp, k_pe, preferred_element_type=jnp.float32)) * scale
        q_idx = blk * Q_BLOCK + jnp.arange(Q_BLOCK)
        mask = key_idx[None, :] <= q_idx[:, None]
        sc = jnp.where(mask, sc, jnp.finfo(jnp.float32).min)
        p = jax.nn.softmax(sc, axis=-1).astype(v.dtype)
        return jnp.einsum('bhqk,bkhd->bqhd', p, v)

    o = lax.map(attend, (qn_blocks, qp_blocks, jnp.arange(n_blk)))
    o = o.swapaxes(0, 1).reshape(b, s, MLA_HEADS * V_HEAD_DIM)
    return o @ w_o


def conformer_conv_branch(pc, dw_w, dw_b, ln_g, ln_b, w_pw, b_pw):
    c = pc[..., :CONV_CH] * jax.nn.sigmoid(pc[..., CONV_CH:])
    c = causal_depthwise_conv(c, dw_w, dw_b)
    c = jax.nn.silu(layer_norm(c, ln_g, ln_b))
    return c @ w_pw + b_pw


def hybrid_mixer(h, positions, w_in, b_in, lru_conv_w, lru_conv_b, lru_w_gate, lru_b_gate,
                 lru_lambda, lru_w_out, q_norm, w_uq, kv_norm, w_ukv, mla_w_o,
                 conv_dw_w, conv_dw_b, conv_ln_g, conv_ln_b, conv_w_out, conv_b_out, w_out):
    proj = h @ w_in + b_in
    o1, o2, o3 = IN_A, IN_A + IN_B, IN_A + IN_B + IN_C
    pa, pb, pc, pg = proj[..., :o1], proj[..., o1:o2], proj[..., o2:o3], proj[..., o3:]
    xa = causal_depthwise_conv(pa[..., :LRU_WIDTH], lru_conv_w, lru_conv_b)
    y_a = (rg_lru(xa, lru_w_gate, lru_b_gate, lru_lambda) * jax.nn.gelu(pa[..., LRU_WIDTH:])) @ lru_w_out
    cq = pb[..., :Q_LORA_RANK]
    ckv = pb[..., Q_LORA_RANK:Q_LORA_RANK + KV_LORA_RANK]
    kpe = pb[..., Q_LORA_RANK + KV_LORA_RANK:]
    y_b = mla_branch(cq, ckv, kpe, positions, q_norm, w_uq, kv_norm, w_ukv, mla_w_o)
    y_c = conformer_conv_branch(pc, conv_dw_w, conv_dw_b, conv_ln_g, conv_ln_b, conv_w_out, conv_b_out)
    gates = jax.nn.sigmoid(pg.astype(jnp.float32)).astype(h.dtype)
    gates = gates.reshape(*pg.shape[:-1], N_BRANCH, D_MODEL)
    merged = gates[..., 0, :] * y_a + gates[..., 1, :] * y_b + gates[..., 2, :] * y_c
    return merged @ w_out


def setup_inputs(seed: int = 0) -> dict:
    key = jax.random.key(seed)
    ks = iter(jax.random.split(key, 48))
    L = DEPTH

    def w(shape, fan_in):
        return jax.random.normal(next(ks), shape, jnp.float32) * fan_in ** -0.5

    def gain(shape):
        return 1.0 + 0.02 * jax.random.normal(next(ks), shape, jnp.float32)

    def bias(shape):
        return 0.02 * jax.random.normal(next(ks), shape, jnp.float32)

    x = jax.random.normal(next(ks), (BATCH, SEQ, D_MODEL), jnp.float32)
    offsets = jax.random.randint(next(ks), (BATCH, 1), 0, MAX_POS_OFFSET, dtype=jnp.int32)
    positions = offsets + jnp.arange(SEQ, dtype=jnp.int32)[None, :]
    u = jax.random.uniform(next(ks), (L, LRU_WIDTH), jnp.float32, 0.9, 0.999)
    a0 = u ** (1.0 / LRU_C)
    lru_lambda = jnp.log(a0) - jnp.log1p(-a0)
    return {
        'x': x,
        'positions': positions,
        'ffn1_norm': gain((L, D_MODEL)),
        'ffn1_w1': w((L, D_MODEL, 2 * D_FF), D_MODEL),
        'ffn1_w2': w((L, D_FF, D_MODEL), D_FF),
        'mix_norm': gain((L, D_MODEL)),
        'w_in': w((L, D_MODEL, D_IN), D_MODEL),
        'b_in': bias((L, D_IN)),
        'lru_conv_w': w((L, LRU_CONV_WIDTH, LRU_WIDTH), LRU_CONV_WIDTH),
        'lru_conv_b': bias((L, LRU_WIDTH)),
        'lru_w_gate': w((L, LRU_HEADS, LRU_HEAD_DIM, 2 * LRU_HEAD_DIM), LRU_HEAD_DIM),
        'lru_b_gate': bias((L, LRU_HEADS, 2 * LRU_HEAD_DIM)),
        'lru_lambda': lru_lambda,
        'lru_w_out': w((L, LRU_WIDTH, D_MODEL), LRU_WIDTH),
        'q_norm': gain((L, Q_LORA_RANK)),
        'w_uq': w((L, Q_LORA_RANK, MLA_HEADS * (QK_NOPE_DIM + QK_ROPE_DIM)), Q_LORA_RANK),
        'kv_norm': gain((L, KV_LORA_RANK)),
        'w_ukv': w((L, KV_LORA_RANK, MLA_HEADS * (QK_NOPE_DIM + V_HEAD_DIM)), KV_LORA_RANK),
        'mla_w_o': w((L, MLA_HEADS * V_HEAD_DIM, D_MODEL), MLA_HEADS * V_HEAD_DIM),
        'conv_dw_w': w((L, CONV_WIDTH, CONV_CH), CONV_WIDTH),
        'conv_dw_b': bias((L, CONV_CH)),
        'conv_ln_g': gain((L, CONV_CH)),
        'conv_ln_b': bias((L, CONV_CH)),
        'conv_w_out': w((L, CONV_CH, D_MODEL), CONV_CH),
        'conv_b_out': bias((L, D_MODEL)),
        'w_out': w((L, D_MODEL, D_MODEL), D_MODEL),
        'ffn2_norm': gain((L, D_MODEL)),
        'ffn2_w1': w((L, D_MODEL, 2 * D_FF), D_MODEL),
        'ffn2_w2': w((L, D_FF, D_MODEL), D_FF),
        'final_norm': gain((D_MODEL,)),
    }


def reference(x, positions, ffn1_norm, ffn1_w1, ffn1_w2, mix_norm, w_in, b_in,
              lru_conv_w, lru_conv_b, lru_w_gate, lru_b_gate, lru_lambda, lru_w_out,
              q_norm, w_uq, kv_norm, w_ukv, mla_w_o,
              conv_dw_w, conv_dw_b, conv_ln_g, conv_ln_b, conv_w_out, conv_b_out,
              w_out, ffn2_norm, ffn2_w1, ffn2_w2, final_norm):
    for l in range(DEPTH):
        x = x + 0.5 * swiglu(rms_norm(x, ffn1_norm[l]), ffn1_w1[l], ffn1_w2[l])
        x = x + hybrid_mixer(
            rms_norm(x, mix_norm[l]), positions, w_in[l], b_in[l],
            lru_conv_w[l], lru_conv_b[l], lru_w_gate[l], lru_b_gate[l], lru_lambda[l], lru_w_out[l],
            q_norm[l], w_uq[l], kv_norm[l], w_ukv[l], mla_w_o[l],
            conv_dw_w[l], conv_dw_b[l], conv_ln_g[l], conv_ln_b[l], conv_w_out[l], conv_b_out[l],
            w_out[l])
        x = x + 0.5 * swiglu(rms_norm(x, ffn2_norm[l]), ffn2_w1[l], ffn2_w2[l])
    return rms_norm(x, final_norm)
```

```python
import functools

import jax
import jax.numpy as jnp
from jax import lax
from jax.experimental import pallas as pl
from jax.experimental.pallas import tpu as pltpu

D_MODEL = 1024
BATCH = 8
SEQ = 2048
DEPTH = 2
D_FF = 2816
NORM_EPS = 1e-6
LRU_WIDTH = 512
LRU_HEADS = 8
LRU_HEAD_DIM = LRU_WIDTH // LRU_HEADS
LRU_CONV_WIDTH = 4
LRU_C = 8.0
MLA_HEADS = 8
QK_NOPE_DIM = 64
QK_ROPE_DIM = 32
V_HEAD_DIM = 64
Q_LORA_RANK = 384
KV_LORA_RANK = 256
ROPE_THETA = 10000.0
CONV_CH = 512
CONV_WIDTH = 31
N_BRANCH = 3
IN_A = 2 * LRU_WIDTH
IN_B = Q_LORA_RANK + KV_LORA_RANK + QK_ROPE_DIM
IN_C = 2 * CONV_CH
IN_G = N_BRANCH * D_MODEL

LANES = 128
SUBLANES = 8
HEAD_PAD = LANES
QK_DIM = QK_NOPE_DIM + QK_ROPE_DIM
N_PAIR = MLA_HEADS // 2
VMEM_LIMIT = 56 * 1024 * 1024

TM_FFN = 512
FF_CHUNK = 256
TM_IN = 256
TM_OUT = 512
SEQ_CHUNK = 256
CONV_PAD = 32
LRU_PAD = SUBLANES
TQ = 256
TK = 256

F32 = jnp.float32
BF16 = jnp.bfloat16


def _const_spec(shape):
    nd = len(shape)
    return pl.BlockSpec(shape, lambda *_: (0,) * nd, pipeline_mode=pl.Buffered(1))


def _params(n_parallel):
    return pltpu.CompilerParams(dimension_semantics=("parallel",) * n_parallel,
                                vmem_limit_bytes=VMEM_LIMIT)


def _rms(x, g):
    return x * lax.rsqrt(jnp.mean(x * x, axis=-1, keepdims=True) + NORM_EPS) * g


def _dot(a, b):
    return jnp.dot(a, b, preferred_element_type=F32)


def _ffn_kernel(x_ref, g_ref, w1_ref, w2_ref, *rest, final):
    if final:
        fg_ref, o_ref = rest
    else:
        (o_ref,) = rest
    x = x_ref[...]
    xn = _rms(x, g_ref[...]).astype(BF16)
    acc = None
    for c in range(D_FF // FF_CHUNK):
        lo = c * FF_CHUNK
        g = _dot(xn, w1_ref[:, lo:lo + FF_CHUNK])
        u = _dot(xn, w1_ref[:, D_FF + lo:D_FF + lo + FF_CHUNK])
        h = (jax.nn.silu(g) * u).astype(BF16)
        d = _dot(h, w2_ref[lo:lo + FF_CHUNK, :])
        acc = d if acc is None else acc + d
    y = x + 0.5 * acc
    if final:
        y = _rms(y, fg_ref[...])
    o_ref[...] = y


def _ffn(x, norm_g, w1, w2, final_g=None):
    tok = x.shape[0]
    final = final_g is not None
    in_specs = [pl.BlockSpec((TM_FFN, D_MODEL), lambda i: (i, 0)),
                _const_spec((1, D_MODEL)), _const_spec((D_MODEL, 2 * D_FF)), _const_spec((D_FF, D_MODEL))]
    args = [x, norm_g, w1, w2]
    if final:
        in_specs.append(_const_spec((1, D_MODEL)))
        args.append(final_g)
    return pl.pallas_call(
        functools.partial(_ffn_kernel, final=final),
        out_shape=jax.ShapeDtypeStruct((tok, D_MODEL), F32),
        grid=(tok // TM_FFN,),
        in_specs=in_specs,
        out_specs=pl.BlockSpec((TM_FFN, D_MODEL), lambda i: (i, 0)),
        compiler_params=_params(1),
        name="ffn_final" if final else "ffn",
    )(*args)


def _inproj_kernel(x_ref, pos_ref, ng_ref, wa_ref, ba_ref, wq_ref, bq_ref, wkv_ref, bkv_ref, wpe_ref, bpe_ref,
                   wc_ref, bc_ref, wg_ref, bg_ref, qn_ref, wuq_ref, kvn_ref, wk_ref, wv_ref, invf_ref,
                   xa_ref, ga_ref, q_ref, k_ref, v_ref, c_ref, gate_ref):
    xn = _rms(x_ref[...], ng_ref[...]).astype(BF16)
    pa = _dot(xn, wa_ref[...]) + ba_ref[...]
    xa_ref[...] = pa[:, :LRU_WIDTH]
    ga_ref[...] = jax.nn.gelu(pa[:, LRU_WIDTH:]).astype(BF16)
    pc = _dot(xn, wc_ref[...]) + bc_ref[...]
    c_ref[...] = pc[:, :CONV_CH] * jax.nn.sigmoid(pc[:, CONV_CH:])
    gate_ref[...] = jax.nn.sigmoid(_dot(xn, wg_ref[...]) + bg_ref[...]).astype(BF16)
    ang = pos_ref[...].astype(F32) * invf_ref[...]
    cosf, sinf = jnp.cos(ang), jnp.sin(ang)
    scale = QK_DIM ** -0.5
    cq = _dot(xn, wq_ref[...]) + bq_ref[...]
    qq = _dot(_rms(cq, qn_ref[...]).astype(BF16), wuq_ref[...])
    nq = MLA_HEADS * HEAD_PAD
    for h in range(MLA_HEADS):
        lo = h * HEAD_PAD
        qh = qq[:, lo:lo + HEAD_PAD] * cosf + qq[:, nq + lo:nq + lo + HEAD_PAD] * sinf
        q_ref[h] = (qh * scale).astype(BF16)
    ckv = _dot(xn, wkv_ref[...]) + bkv_ref[...]
    ckvn = _rms(ckv, kvn_ref[...]).astype(BF16)
    kk = _dot(ckvn, wk_ref[...])
    vv = _dot(ckvn, wv_ref[...])
    pe = _dot(xn, wpe_ref[...]) + bpe_ref[...]
    kpe = pe[:, :HEAD_PAD] * cosf + pe[:, HEAD_PAD:] * sinf
    for h in range(MLA_HEADS):
        k_ref[h] = (kk[:, h * HEAD_PAD:(h + 1) * HEAD_PAD] + kpe).astype(BF16)
    for p in range(N_PAIR):
        v_ref[p] = vv[:, p * LANES:(p + 1) * LANES].astype(BF16)


def _inproj(x, pos, w):
    nt = SEQ // TM_IN
    tile = lambda n: pl.BlockSpec((None, TM_IN, n), lambda b, i: (b, i, 0))
    heads = lambda n: pl.BlockSpec((None, n, TM_IN, LANES), lambda b, i: (b, 0, i, 0))
    consts = [w["mix_norm"], w["wa"], w["ba"], w["wq"], w["bq"], w["wkv"], w["bkv"], w["wpe"], w["bpe"],
              w["wc"], w["bc"], w["wg"], w["bg"], w["q_norm"], w["wuq"], w["kv_norm"], w["wk"], w["wv"], w["invf"]]
    return pl.pallas_call(
        _inproj_kernel,
        out_shape=(jax.ShapeDtypeStruct((BATCH, SEQ, LRU_WIDTH), F32),
                   jax.ShapeDtypeStruct((BATCH, SEQ, LRU_WIDTH), BF16),
                   jax.ShapeDtypeStruct((BATCH, MLA_HEADS, SEQ, HEAD_PAD), BF16),
                   jax.ShapeDtypeStruct((BATCH, MLA_HEADS, SEQ, HEAD_PAD), BF16),
                   jax.ShapeDtypeStruct((BATCH, N_PAIR, SEQ, LANES), BF16),
                   jax.ShapeDtypeStruct((BATCH, SEQ, CONV_CH), F32),
                   jax.ShapeDtypeStruct((BATCH, SEQ, IN_G), BF16)),
        grid=(BATCH, nt),
        in_specs=[tile(D_MODEL), tile(1)] + [_const_spec(c.shape) for c in consts],
        out_specs=(tile(LRU_WIDTH), tile(LRU_WIDTH), heads(MLA_HEADS), heads(MLA_HEADS), heads(N_PAIR),
                   tile(CONV_CH), tile(IN_G)),
        compiler_params=_params(2),
        name="inproj",
    )(x, pos, *consts)


def _seqmix_kernel(xa_ref, ga_ref, c_ref, cw_ref, cb_ref, wbd_ref, bgate_ref, lam_ref, dw_ref, db_ref,
                   lng_ref, lnb_ref, ya_ref, yc_ref, xpad, cpad, a_s, u_s):
    xpad[pl.ds(0, LRU_PAD), :] = jnp.zeros((LRU_PAD, LRU_WIDTH), F32)
    xpad[pl.ds(LRU_PAD, SEQ), :] = xa_ref[...]
    cpad[pl.ds(0, CONV_PAD), :] = jnp.zeros((CONV_PAD, CONV_CH), F32)
    cpad[pl.ds(CONV_PAD, SEQ), :] = c_ref[...]
    sp = jax.nn.softplus(-lam_ref[...])
    row = lax.broadcasted_iota(jnp.int32, (SUBLANES, LRU_WIDTH), 0)
    R = SEQ_CHUNK

    def chunk(i, hprev):
        base = pl.multiple_of(i * R, R)
        win = xpad[pl.ds(base, R + LRU_PAD), :]
        xa = cb_ref[...]
        for j in range(LRU_CONV_WIDTH):
            s = LRU_PAD - (LRU_CONV_WIDTH - 1) + j
            xa = xa + cw_ref[j:j + 1, :] * win[s:s + R, :]
        gates = _dot(xa.astype(BF16), wbd_ref[...]) + bgate_ref[...]
        r = jax.nn.sigmoid(gates[:, :LRU_WIDTH])
        ig = jax.nn.sigmoid(gates[:, LRU_WIDTH:])
        log_a = (-LRU_C) * r * sp
        a = jnp.exp(log_a)
        e = a * a
        y2 = 2.0 * log_a
        one_m = jnp.where(e == 1.0, -y2, (1.0 - e) * y2 / jnp.log(e))
        a_s[pl.ds(base, R), :] = a
        u_s[pl.ds(base, R), :] = jnp.sqrt(one_m) * (ig * xa)

        def blk(k, h):
            b8 = pl.multiple_of(base + k * SUBLANES, SUBLANES)
            av = a_s[pl.ds(b8, SUBLANES), :]
            hv = u_s[pl.ds(b8, SUBLANES), :]
            for d in (1, 2, 4):
                keep = row >= d
                hv = jnp.where(keep, av * pltpu.roll(hv, d, axis=0) + hv, hv)
                av = jnp.where(keep, av * pltpu.roll(av, d, axis=0), av)
            hv = hv + av * h
            u_s[pl.ds(b8, SUBLANES), :] = hv
            return jnp.broadcast_to(hv[SUBLANES - 1:SUBLANES, :], (SUBLANES, LRU_WIDTH))

        hnext = lax.fori_loop(0, R // SUBLANES, blk, hprev)
        ya_ref[pl.ds(base, R), :] = (u_s[pl.ds(base, R), :] * ga_ref[pl.ds(base, R), :].astype(F32)).astype(BF16)

        cwin = cpad[pl.ds(base, R + CONV_PAD), :]
        acc = db_ref[...]
        for j in range(CONV_WIDTH):
            s = CONV_PAD - (CONV_WIDTH - 1) + j
            acc = acc + dw_ref[j:j + 1, :] * cwin[s:s + R, :]
        mu = jnp.mean(acc, axis=-1, keepdims=True)
        dlt = acc - mu
        var = jnp.mean(dlt * dlt, axis=-1, keepdims=True)
        yn = dlt * lax.rsqrt(var + NORM_EPS) * lng_ref[...] + lnb_ref[...]
        yc_ref[pl.ds(base, R), :] = jax.nn.silu(yn).astype(BF16)
        return hnext

    lax.fori_loop(0, SEQ // R, chunk, jnp.zeros((SUBLANES, LRU_WIDTH), F32))


def _seqmix(xa, ga, c, w):
    seq = lambda: pl.BlockSpec((None, SEQ, LRU_WIDTH), lambda b: (b, 0, 0))
    consts = [w["lru_conv_w"], w["lru_conv_b"], w["wbd"], w["bgate"], w["lam"], w["conv_dw_w"], w["conv_dw_b"],
              w["conv_ln_g"], w["conv_ln_b"]]
    return pl.pallas_call(
        _seqmix_kernel,
        out_shape=(jax.ShapeDtypeStruct((BATCH, SEQ, LRU_WIDTH), BF16),
                   jax.ShapeDtypeStruct((BATCH, SEQ, CONV_CH), BF16)),
        grid=(BATCH,),
        in_specs=[seq(), seq(), seq()] + [_const_spec(c_.shape) for c_ in consts],
        out_specs=(seq(), seq()),
        scratch_shapes=[pltpu.VMEM((SEQ + LRU_PAD, LRU_WIDTH), F32), pltpu.VMEM((SEQ + CONV_PAD, CONV_CH), F32),
                        pltpu.VMEM((SEQ, LRU_WIDTH), F32), pltpu.VMEM((SEQ, LRU_WIDTH), F32)],
        compiler_params=_params(1),
        name="seqmix",
    )(xa, ga, c, *consts)


def _attn_kernel(q_ref, k_ref, v_ref, o_ref, m_s, l_s, acc_s):
    qi = pl.program_id(2)
    m_s[...] = jnp.full(m_s.shape, -jnp.inf, F32)
    l_s[...] = jnp.zeros(l_s.shape, F32)
    acc_s[...] = jnp.zeros(acc_s.shape, F32)
    rep = TK // LANES

    def step(j, masked):
        k0 = pl.multiple_of(j * TK, TK)
        v = v_ref[pl.ds(k0, TK), :]
        for h in range(2):
            s = lax.dot_general(q_ref[h], k_ref[h, pl.ds(k0, TK), :], (((1,), (1,)), ((), ())),
                                preferred_element_type=F32)
            if masked:
                qpos = lax.broadcasted_iota(jnp.int32, (TQ, TK), 0)
                kpos = lax.broadcasted_iota(jnp.int32, (TQ, TK), 1)
                s = jnp.where(kpos <= qpos, s, -jnp.inf)
            m_prev = m_s[h]
            m_new = jnp.maximum(m_prev, jnp.max(s, axis=1, keepdims=True))
            alpha = jnp.exp(m_prev - m_new)
            p = jnp.exp(s - jnp.concatenate([m_new] * rep, axis=1))
            l_s[h] = alpha * l_s[h] + jnp.sum(p, axis=1, keepdims=True)
            acc_s[h] = alpha * acc_s[h] + _dot(p.astype(BF16), v)
            m_s[h] = m_new

    def body(j, carry):
        step(j, False)
        return carry

    lax.fori_loop(0, qi, body, 0)
    step(qi, True)
    lane = lax.broadcasted_iota(jnp.int32, (TQ, LANES), 1)
    o_ref[...] = jnp.where(lane < V_HEAD_DIM, acc_s[0] / l_s[0], acc_s[1] / l_s[1]).astype(BF16)


def _attention(q, k, v):
    assert TQ == TK
    return pl.pallas_call(
        _attn_kernel,
        out_shape=jax.ShapeDtypeStruct((BATCH, SEQ, MLA_HEADS * V_HEAD_DIM), BF16),
        grid=(BATCH, N_PAIR, SEQ // TQ),
        in_specs=[pl.BlockSpec((None, 2, TQ, HEAD_PAD), lambda b, p, i: (b, p, i, 0)),
                  pl.BlockSpec((None, 2, SEQ, HEAD_PAD), lambda b, p, i: (b, p, 0, 0)),
                  pl.BlockSpec((None, None, SEQ, LANES), lambda b, p, i: (b, p, 0, 0))],
        out_specs=pl.BlockSpec((None, TQ, LANES), lambda b, p, i: (b, i, p)),
        scratch_shapes=[pltpu.VMEM((2, TQ, LANES), F32)] * 3,
        compiler_params=_params(3),
        name="attention",
    )(q, k, v)


def _outproj_kernel(x_ref, ya_ref, ob_ref, yc_ref, gate_ref, wa_ref, wb_ref, wc_ref, bc_ref, wo_ref, o_ref):
    y_a = _dot(ya_ref[...], wa_ref[...])
    y_b = _dot(ob_ref[...], wb_ref[...])
    y_c = _dot(yc_ref[...], wc_ref[...]) + bc_ref[...]
    g = gate_ref[...].astype(F32)
    merged = g[:, :D_MODEL] * y_a + g[:, D_MODEL:2 * D_MODEL] * y_b + g[:, 2 * D_MODEL:] * y_c
    o_ref[...] = x_ref[...] + _dot(merged.astype(BF16), wo_ref[...])


def _outproj(x, ya, ob, yc, gates, w):
    tok = x.shape[0]
    tile = lambda n: pl.BlockSpec((TM_OUT, n), lambda i: (i, 0))
    consts = [w["lru_w_out"], w["mla_w_o"], w["conv_w_out"], w["conv_b_out"], w["w_out"]]
    return pl.pallas_call(
        _outproj_kernel,
        out_shape=jax.ShapeDtypeStruct((tok, D_MODEL), F32),
        grid=(tok // TM_OUT,),
        in_specs=[tile(D_MODEL), tile(LRU_WIDTH), tile(MLA_HEADS * V_HEAD_DIM), tile(CONV_CH), tile(IN_G)]
        + [_const_spec(c.shape) for c in consts],
        out_specs=tile(D_MODEL),
        compiler_params=_params(1),
        name="outproj",
    )(x, ya, ob, yc, gates, *consts)


def _prep_layer(p):
    row = lambda v: v.reshape(1, -1).astype(F32)
    o1, o2, o3 = IN_A, IN_A + IN_B, IN_A + IN_B + IN_C
    w_in, b_in = p["w_in"], p["b_in"]
    oq, okv, ope = o1, o1 + Q_LORA_RANK, o1 + Q_LORA_RANK + KV_LORA_RANK
    half = QK_ROPE_DIM // 2

    def rope_cols(m, neg_first):
        z = lambda n: jnp.zeros(m.shape[:-1] + (n,), m.dtype)
        plain = jnp.concatenate([z(QK_NOPE_DIM), m, z(HEAD_PAD - QK_DIM)], axis=-1)
        rot = jnp.concatenate([z(QK_NOPE_DIM), -m[..., half:], m[..., :half], z(HEAD_PAD - QK_DIM)], axis=-1)
        return plain, rot

    wpe_plain, wpe_rot = rope_cols(w_in[:, ope:o2], True)
    bpe_plain, bpe_rot = rope_cols(b_in[ope:o2], True)

    w_uq = p["w_uq"].reshape(Q_LORA_RANK, MLA_HEADS, QK_DIM)
    zq = jnp.zeros((Q_LORA_RANK, MLA_HEADS, HEAD_PAD - QK_DIM), F32)
    q_plain = jnp.concatenate([w_uq, zq], axis=-1)
    q_pe = w_uq[..., QK_NOPE_DIM:]
    q_rot = jnp.concatenate([jnp.zeros((Q_LORA_RANK, MLA_HEADS, QK_NOPE_DIM), F32), -q_pe[..., half:], q_pe[..., :half],
                             zq], axis=-1)
    wuq = jnp.concatenate([q_plain.reshape(Q_LORA_RANK, -1), q_rot.reshape(Q_LORA_RANK, -1)], axis=-1)

    w_ukv = p["w_ukv"].reshape(KV_LORA_RANK, MLA_HEADS, QK_NOPE_DIM + V_HEAD_DIM)
    wk = jnp.concatenate([w_ukv[..., :QK_NOPE_DIM],
                          jnp.zeros((KV_LORA_RANK, MLA_HEADS, HEAD_PAD - QK_NOPE_DIM), F32)], axis=-1)
    wv = w_ukv[..., QK_NOPE_DIM:]

    wg4 = p["lru_w_gate"]
    eye = jnp.eye(LRU_HEADS, dtype=F32)
    bd = lambda blk: jnp.einsum("hde,hg->hdge", blk, eye).reshape(LRU_WIDTH, LRU_WIDTH)
    wbd = jnp.concatenate([bd(wg4[..., :LRU_HEAD_DIM]), bd(wg4[..., LRU_HEAD_DIM:])], axis=-1)
    bgate = jnp.concatenate([p["lru_b_gate"][:, :LRU_HEAD_DIM].reshape(-1),
                             p["lru_b_gate"][:, LRU_HEAD_DIM:].reshape(-1)])

    inv_freq = ROPE_THETA ** (-jnp.arange(0, QK_ROPE_DIM, 2, dtype=F32) / QK_ROPE_DIM)
    invf = jnp.concatenate([jnp.zeros((QK_NOPE_DIM,), F32), inv_freq, inv_freq,
                            jnp.zeros((HEAD_PAD - QK_DIM,), F32)])

    return dict(
        ffn1_norm=row(p["ffn1_norm"]), ffn1_w1=p["ffn1_w1"].astype(BF16), ffn1_w2=p["ffn1_w2"].astype(BF16),
        ffn2_norm=row(p["ffn2_norm"]), ffn2_w1=p["ffn2_w1"].astype(BF16), ffn2_w2=p["ffn2_w2"].astype(BF16),
        mix_norm=row(p["mix_norm"]),
        wa=w_in[:, :o1].astype(BF16), ba=row(b_in[:o1]),
        wq=w_in[:, oq:okv].astype(BF16), bq=row(b_in[oq:okv]),
        wkv=w_in[:, okv:ope].astype(BF16), bkv=row(b_in[okv:ope]),
        wpe=jnp.concatenate([wpe_plain, wpe_rot], axis=-1).astype(BF16),
        bpe=row(jnp.concatenate([bpe_plain, bpe_rot])),
        wc=w_in[:, o2:o3].astype(BF16), bc=row(b_in[o2:o3]),
        wg=w_in[:, o3:].astype(BF16), bg=row(b_in[o3:]),
        q_norm=row(p["q_norm"]), wuq=wuq.astype(BF16), kv_norm=row(p["kv_norm"]),
        wk=wk.reshape(KV_LORA_RANK, -1).astype(BF16), wv=wv.reshape(KV_LORA_RANK, -1).astype(BF16),
        invf=row(invf),
        lru_conv_w=p["lru_conv_w"].astype(F32), lru_conv_b=row(p["lru_conv_b"]),
        wbd=wbd.astype(BF16), bgate=row(bgate), lam=row(p["lru_lambda"]),
        conv_dw_w=p["conv_dw_w"].astype(F32), conv_dw_b=row(p["conv_dw_b"]),
        conv_ln_g=row(p["conv_ln_g"]), conv_ln_b=row(p["conv_ln_b"]),
        lru_w_out=p["lru_w_out"].astype(BF16), mla_w_o=p["mla_w_o"].astype(BF16),
        conv_w_out=p["conv_w_out"].astype(BF16), conv_b_out=row(p["conv_b_out"]), w_out=p["w_out"].astype(BF16),
    )


def kernel(x, positions, ffn1_norm, ffn1_w1, ffn1_w2, mix_norm, w_in, b_in, lru_conv_w, lru_conv_b, lru_w_gate, lru_b_gate, lru_lambda, lru_w_out, q_norm, w_uq, kv_norm, w_ukv, mla_w_o, conv_dw_w, conv_dw_b, conv_ln_g, conv_ln_b, conv_w_out, conv_b_out, w_out, ffn2_norm, ffn2_w1, ffn2_w2, final_norm):
    stacked = dict(ffn1_norm=ffn1_norm, ffn1_w1=ffn1_w1, ffn1_w2=ffn1_w2, mix_norm=mix_norm, w_in=w_in, b_in=b_in,
                   lru_conv_w=lru_conv_w, lru_conv_b=lru_conv_b, lru_w_gate=lru_w_gate, lru_b_gate=lru_b_gate,
                   lru_lambda=lru_lambda, lru_w_out=lru_w_out, q_norm=q_norm, w_uq=w_uq, kv_norm=kv_norm,
                   w_ukv=w_ukv, mla_w_o=mla_w_o, conv_dw_w=conv_dw_w, conv_dw_b=conv_dw_b, conv_ln_g=conv_ln_g,
                   conv_ln_b=conv_ln_b, conv_w_out=conv_w_out, conv_b_out=conv_b_out, w_out=w_out,
                   ffn2_norm=ffn2_norm, ffn2_w1=ffn2_w1, ffn2_w2=ffn2_w2)
    tok = BATCH * SEQ
    pos = positions.reshape(BATCH, SEQ, 1)
    final_g = final_norm.reshape(1, D_MODEL).astype(F32)
    xf = x.reshape(tok, D_MODEL)
    for l in range(DEPTH):
        w = _prep_layer({k: v[l] for k, v in stacked.items()})
        xf = _ffn(xf, w["ffn1_norm"], w["ffn1_w1"], w["ffn1_w2"])
        xa, ga, q, k, v, c, gates = _inproj(xf.reshape(BATCH, SEQ, D_MODEL), pos, w)
        ya, yc = _seqmix(xa, ga, c, w)
        ob = _attention(q, k, v)
        xf = _outproj(xf, ya.reshape(tok, LRU_WIDTH), ob.reshape(tok, MLA_HEADS * V_HEAD_DIM),
                      yc.reshape(tok, CONV_CH), gates.reshape(tok, IN_G), w)
        xf = _ffn(xf, w["ffn2_norm"], w["ffn2_w1"], w["ffn2_w2"], final_g if l == DEPTH - 1 else None)
    return xf.reshape(BATCH, SEQ, D_MODEL)
```

```python
import functools

import jax
import jax.numpy as jnp
from jax import lax
from jax.experimental import pallas as pl
from jax.experimental.pallas import tpu as pltpu

D_MODEL = 1024
BATCH = 8
SEQ = 2048
DEPTH = 2
D_FF = 2816
NORM_EPS = 1e-6
LRU_WIDTH = 512
LRU_HEADS = 8
LRU_HEAD_DIM = LRU_WIDTH // LRU_HEADS
LRU_CONV_WIDTH = 4
LRU_C = 8.0
MLA_HEADS = 8
QK_NOPE_DIM = 64
QK_ROPE_DIM = 32
V_HEAD_DIM = 64
Q_LORA_RANK = 384
KV_LORA_RANK = 256
ROPE_THETA = 10000.0
CONV_CH = 512
CONV_WIDTH = 31
N_BRANCH = 3
IN_A = 2 * LRU_WIDTH
IN_B = Q_LORA_RANK + KV_LORA_RANK + QK_ROPE_DIM
IN_C = 2 * CONV_CH
IN_G = N_BRANCH * D_MODEL

LANES = 128
SUBLANES = 8
HEAD_PAD = LANES
QK_DIM = QK_NOPE_DIM + QK_ROPE_DIM
LOG2_E = 1.4426950408889634
VMEM_LIMIT = 56 * 1024 * 1024

TM_FFN = 512
FF_CHUNK = 256
TM_IN = 256
TM_OUT = 512
SEQ_CHUNK = 256
CONV_PAD = 32
LRU_PAD = SUBLANES
TQ = 256
TK = 256

F32 = jnp.float32
BF16 = jnp.bfloat16


def _const_spec(shape):
    nd = len(shape)
    return pl.BlockSpec(shape, lambda *_: (0,) * nd, pipeline_mode=pl.Buffered(1))


def _params(n_parallel):
    return pltpu.CompilerParams(dimension_semantics=("parallel",) * n_parallel,
                                vmem_limit_bytes=VMEM_LIMIT)


def _rms(x, g):
    return x * lax.rsqrt(jnp.mean(x * x, axis=-1, keepdims=True) + NORM_EPS) * g


def _dot(a, b):
    return jnp.dot(a, b, preferred_element_type=F32)


def _ffn_kernel(x_ref, g_ref, w1_ref, w2_ref, *rest, final):
    if final:
        fg_ref, o_ref = rest
    else:
        (o_ref,) = rest
    x = x_ref[...]
    xn = _rms(x, g_ref[...]).astype(BF16)
    acc = None
    for c in range(D_FF // FF_CHUNK):
        lo = c * FF_CHUNK
        g = _dot(xn, w1_ref[:, lo:lo + FF_CHUNK])
        u = _dot(xn, w1_ref[:, D_FF + lo:D_FF + lo + FF_CHUNK])
        h = (jax.nn.silu(g) * u).astype(BF16)
        d = _dot(h, w2_ref[lo:lo + FF_CHUNK, :])
        acc = d if acc is None else acc + d
    y = x + 0.5 * acc
    if final:
        y = _rms(y, fg_ref[...])
    o_ref[...] = y


def _ffn(x, norm_g, w1, w2, final_g=None):
    tok = x.shape[0]
    final = final_g is not None
    in_specs = [pl.BlockSpec((TM_FFN, D_MODEL), lambda i: (i, 0)),
                _const_spec((1, D_MODEL)), _const_spec((D_MODEL, 2 * D_FF)), _const_spec((D_FF, D_MODEL))]
    args = [x, norm_g, w1, w2]
    if final:
        in_specs.append(_const_spec((1, D_MODEL)))
        args.append(final_g)
    return pl.pallas_call(
        functools.partial(_ffn_kernel, final=final),
        out_shape=jax.ShapeDtypeStruct((tok, D_MODEL), F32),
        grid=(tok // TM_FFN,),
        in_specs=in_specs,
        out_specs=pl.BlockSpec((TM_FFN, D_MODEL), lambda i: (i, 0)),
        compiler_params=_params(1),
        name="ffn_final" if final else "ffn",
    )(*args)


def _inproj_kernel(x_ref, pos_ref, ng_ref, wa_ref, ba_ref, wq_ref, bq_ref, wkv_ref, bkv_ref, wpe_ref, bpe_ref,
                   wc_ref, bc_ref, wg_ref, bg_ref, qn_ref, wuq_ref, kvn_ref, wk_ref, wv_ref, invf_ref,
                   xa_ref, ga_ref, q_ref, k_ref, v_ref, c_ref, gate_ref):
    xn = _rms(x_ref[...], ng_ref[...]).astype(BF16)
    pa = _dot(xn, wa_ref[...]) + ba_ref[...]
    xa_ref[...] = pa[:, :LRU_WIDTH]
    ga_ref[...] = jax.nn.gelu(pa[:, LRU_WIDTH:]).astype(BF16)
    pc = _dot(xn, wc_ref[...]) + bc_ref[...]
    c_ref[...] = pc[:, :CONV_CH] * jax.nn.sigmoid(pc[:, CONV_CH:])
    gate_ref[...] = jax.nn.sigmoid(_dot(xn, wg_ref[...]) + bg_ref[...]).astype(BF16)
    ang = pos_ref[...].astype(F32) * invf_ref[...]
    cosf, sinf = jnp.cos(ang), jnp.sin(ang)
    scale = QK_DIM ** -0.5 * LOG2_E
    cq = _dot(xn, wq_ref[...]) + bq_ref[...]
    qq = _dot(_rms(cq, qn_ref[...]).astype(BF16), wuq_ref[...])
    nq = MLA_HEADS * HEAD_PAD
    for h in range(MLA_HEADS):
        lo = h * HEAD_PAD
        qh = qq[:, lo:lo + HEAD_PAD] * cosf + qq[:, nq + lo:nq + lo + HEAD_PAD] * sinf
        q_ref[h] = (qh * scale).astype(BF16)
    ckv = _dot(xn, wkv_ref[...]) + bkv_ref[...]
    ckvn = _rms(ckv, kvn_ref[...]).astype(BF16)
    kk = _dot(ckvn, wk_ref[...])
    vv = _dot(ckvn, wv_ref[...])
    pe = _dot(xn, wpe_ref[...]) + bpe_ref[...]
    kpe = pe[:, :HEAD_PAD] * cosf + pe[:, HEAD_PAD:] * sinf
    for h in range(MLA_HEADS):
        k_ref[h] = (kk[:, h * HEAD_PAD:(h + 1) * HEAD_PAD] + kpe).astype(BF16)
    ones_lane = lax.broadcasted_iota(jnp.int32, (TM_IN, HEAD_PAD), 1) == V_HEAD_DIM
    for h in range(MLA_HEADS):
        v_ref[h] = jnp.where(ones_lane, 1.0, vv[:, h * HEAD_PAD:(h + 1) * HEAD_PAD]).astype(BF16)


def _inproj(x, pos, w):
    nt = SEQ // TM_IN
    tile = lambda n: pl.BlockSpec((None, TM_IN, n), lambda b, i: (b, i, 0))
    heads = lambda n: pl.BlockSpec((None, n, TM_IN, LANES), lambda b, i: (b, 0, i, 0))
    consts = [w["mix_norm"], w["wa"], w["ba"], w["wq"], w["bq"], w["wkv"], w["bkv"], w["wpe"], w["bpe"],
              w["wc"], w["bc"], w["wg"], w["bg"], w["q_norm"], w["wuq"], w["kv_norm"], w["wk"], w["wv"], w["invf"]]
    return pl.pallas_call(
        _inproj_kernel,
        out_shape=(jax.ShapeDtypeStruct((BATCH, SEQ, LRU_WIDTH), F32),
                   jax.ShapeDtypeStruct((BATCH, SEQ, LRU_WIDTH), BF16),
                   jax.ShapeDtypeStruct((BATCH, MLA_HEADS, SEQ, HEAD_PAD), BF16),
                   jax.ShapeDtypeStruct((BATCH, MLA_HEADS, SEQ, HEAD_PAD), BF16),
                   jax.ShapeDtypeStruct((BATCH, MLA_HEADS, SEQ, HEAD_PAD), BF16),
                   jax.ShapeDtypeStruct((BATCH, SEQ, CONV_CH), F32),
                   jax.ShapeDtypeStruct((BATCH, SEQ, IN_G), BF16)),
        grid=(BATCH, nt),
        in_specs=[tile(D_MODEL), tile(1)] + [_const_spec(c.shape) for c in consts],
        out_specs=(tile(LRU_WIDTH), tile(LRU_WIDTH), heads(MLA_HEADS), heads(MLA_HEADS), heads(MLA_HEADS),
                   tile(CONV_CH), tile(IN_G)),
        compiler_params=_params(2),
        name="inproj",
    )(x, pos, *consts)


def _causal_taps(win, w_ref, n_taps, pad, rows):
    n = win.shape[0]
    offs = [pad - (n_taps - 1) + j for j in range(n_taps)]
    acc = None
    for r in range(SUBLANES):
        taps = [j for j in range(n_taps) if offs[j] % SUBLANES == r]
        if not taps:
            continue
        shifted = win if r == 0 else pltpu.roll(win, n - r, axis=0)
        for j in taps:
            lo = offs[j] - r
            term = w_ref[j:j + 1, :] * shifted[lo:lo + rows, :]
            acc = term if acc is None else acc + term
    return acc


def _seqmix_kernel(xa_ref, ga_ref, c_ref, cw_ref, cb_ref, wbd_ref, bgate_ref, lam_ref, dw_ref, db_ref,
                   lng_ref, lnb_ref, ya_ref, yc_ref, xpad, cpad, a_s, u_s):
    xpad[pl.ds(0, LRU_PAD), :] = jnp.zeros((LRU_PAD, LRU_WIDTH), F32)
    xpad[pl.ds(LRU_PAD, SEQ), :] = xa_ref[...]
    cpad[pl.ds(0, CONV_PAD), :] = jnp.zeros((CONV_PAD, CONV_CH), F32)
    cpad[pl.ds(CONV_PAD, SEQ), :] = c_ref[...]
    sp = jax.nn.softplus(-lam_ref[...])
    row = lax.broadcasted_iota(jnp.int32, (SUBLANES, LRU_WIDTH), 0)
    R = SEQ_CHUNK

    def chunk(i, hprev):
        base = pl.multiple_of(i * R, R)
        xa = cb_ref[...] + _causal_taps(xpad[pl.ds(base, R + LRU_PAD), :], cw_ref, LRU_CONV_WIDTH, LRU_PAD, R)
        gates = _dot(xa.astype(BF16), wbd_ref[...]) + bgate_ref[...]
        r = jax.nn.sigmoid(gates[:, :LRU_WIDTH])
        ig = jax.nn.sigmoid(gates[:, LRU_WIDTH:])
        log_a = (-LRU_C) * r * sp
        a = jnp.exp(log_a)
        e = a * a
        y2 = 2.0 * log_a
        one_m = jnp.where(e == 1.0, -y2, (1.0 - e) * y2 / jnp.log(e))
        a_s[pl.ds(base, R), :] = a
        u_s[pl.ds(base, R), :] = jnp.sqrt(one_m) * (ig * xa)

        def blk(k, h):
            b8 = pl.multiple_of(base + k * SUBLANES, SUBLANES)
            av = a_s[pl.ds(b8, SUBLANES), :]
            hv = u_s[pl.ds(b8, SUBLANES), :]
            for d in (1, 2, 4):
                keep = row >= d
                hv = jnp.where(keep, av * pltpu.roll(hv, d, axis=0) + hv, hv)
                av = jnp.where(keep, av * pltpu.roll(av, d, axis=0), av)
            hv = hv + av * h
            u_s[pl.ds(b8, SUBLANES), :] = hv
            return jnp.broadcast_to(hv[SUBLANES - 1:SUBLANES, :], (SUBLANES, LRU_WIDTH))

        hnext = lax.fori_loop(0, R // SUBLANES, blk, hprev)
        ya_ref[pl.ds(base, R), :] = (u_s[pl.ds(base, R), :] * ga_ref[pl.ds(base, R), :].astype(F32)).astype(BF16)

        acc = db_ref[...] + _causal_taps(cpad[pl.ds(base, R + CONV_PAD), :], dw_ref, CONV_WIDTH, CONV_PAD, R)
        mu = jnp.mean(acc, axis=-1, keepdims=True)
        dlt = acc - mu
        var = jnp.mean(dlt * dlt, axis=-1, keepdims=True)
        yn = dlt * lax.rsqrt(var + NORM_EPS) * lng_ref[...] + lnb_ref[...]
        yc_ref[pl.ds(base, R), :] = jax.nn.silu(yn).astype(BF16)
        return hnext

    lax.fori_loop(0, SEQ // R, chunk, jnp.zeros((SUBLANES, LRU_WIDTH), F32))


def _seqmix(xa, ga, c, w):
    seq = lambda: pl.BlockSpec((None, SEQ, LRU_WIDTH), lambda b: (b, 0, 0))
    consts = [w["lru_conv_w"], w["lru_conv_b"], w["wbd"], w["bgate"], w["lam"], w["conv_dw_w"], w["conv_dw_b"],
              w["conv_ln_g"], w["conv_ln_b"]]
    return pl.pallas_call(
        _seqmix_kernel,
        out_shape=(jax.ShapeDtypeStruct((BATCH, SEQ, LRU_WIDTH), BF16),
                   jax.ShapeDtypeStruct((BATCH, SEQ, CONV_CH), BF16)),
        grid=(BATCH,),
        in_specs=[seq(), seq(), seq()] + [_const_spec(c_.shape) for c_ in consts],
        out_specs=(seq(), seq()),
        scratch_shapes=[pltpu.VMEM((SEQ + LRU_PAD, LRU_WIDTH), F32), pltpu.VMEM((SEQ + CONV_PAD, CONV_CH), F32),
                        pltpu.VMEM((SEQ, LRU_WIDTH), F32), pltpu.VMEM((SEQ, LRU_WIDTH), F32)],
        compiler_params=_params(1),
        name="seqmix",
    )(xa, ga, c, *consts)


def _attn_kernel(q_ref, k_ref, v_ref, o_ref, s_scr, mx_scr, acc_scr):
    qi = pl.program_id(1)
    groups = TK // LANES
    row = lax.broadcasted_iota(jnp.int32, (TQ, TK), 0)
    col = lax.broadcasted_iota(jnp.int32, (TQ, TK), 1)
    mx_scr[...] = jnp.full(mx_scr.shape, -jnp.inf, F32)

    def scores(j, diagonal):
        k0 = pl.multiple_of(j * TK, TK)
        for h in range(MLA_HEADS):
            s = lax.dot_general(q_ref[h], k_ref[h, pl.ds(k0, TK), :], (((1,), (1,)), ((), ())),
                                preferred_element_type=F32)
            if diagonal:
                s = jnp.where(col <= row, s, -jnp.inf)
            s_scr[h, j] = s
            mx = mx_scr[h]
            for g in range(groups):
                mx = jnp.maximum(mx, s[:, g * LANES:(g + 1) * LANES])
            mx_scr[h] = mx

    def pairwise(n, fn):
        def two(jj, carry):
            fn(2 * jj)
            fn(2 * jj + 1)
            return carry

        lax.fori_loop(0, n // 2, two, 0)

        @pl.when(n % 2 == 1)
        def _():
            fn(n - 1)

    pairwise(qi, lambda j: scores(j, False))
    scores(qi, True)

    for h in range(MLA_HEADS):
        mx_scr[h] = jnp.broadcast_to(jnp.max(mx_scr[h], axis=1, keepdims=True), (TQ, LANES))
    acc_scr[...] = jnp.zeros(acc_scr.shape, F32)

    def accumulate(j):
        k0 = pl.multiple_of(j * TK, TK)
        for h in range(MLA_HEADS):
            p = jnp.exp2(s_scr[h, j] - jnp.concatenate([mx_scr[h]] * groups, axis=1))
            acc_scr[h] += _dot(p.astype(BF16), v_ref[h, pl.ds(k0, TK), :])

    pairwise(qi + 1, accumulate)

    outs = []
    for h in range(MLA_HEADS):
        acc = acc_scr[h]
        outs.append(acc[:, :V_HEAD_DIM] / acc[:, V_HEAD_DIM:V_HEAD_DIM + 1])
    o_ref[...] = jnp.concatenate(outs, axis=1).astype(BF16)


def _attention(q, k, v):
    assert TQ == TK
    stat = pltpu.VMEM((MLA_HEADS, TQ, LANES), F32)
    return pl.pallas_call(
        _attn_kernel,
        out_shape=jax.ShapeDtypeStruct((BATCH, SEQ, MLA_HEADS * V_HEAD_DIM), BF16),
        grid=(BATCH, SEQ // TQ),
        in_specs=[pl.BlockSpec((None, MLA_HEADS, TQ, HEAD_PAD), lambda b, i: (b, 0, i, 0)),
                  pl.BlockSpec((None, MLA_HEADS, SEQ, HEAD_PAD), lambda b, i: (b, 0, 0, 0)),
                  pl.BlockSpec((None, MLA_HEADS, SEQ, HEAD_PAD), lambda b, i: (b, 0, 0, 0))],
        out_specs=pl.BlockSpec((None, TQ, MLA_HEADS * V_HEAD_DIM), lambda b, i: (b, i, 0)),
        scratch_shapes=[pltpu.VMEM((MLA_HEADS, SEQ // TK, TQ, TK), F32), stat, stat],
        compiler_params=_params(2),
        name="attention",
    )(q, k, v)


def _outproj_kernel(x_ref, ya_ref, ob_ref, yc_ref, gate_ref, wa_ref, wb_ref, wc_ref, bc_ref, wo_ref, o_ref):
    y_a = _dot(ya_ref[...], wa_ref[...])
    y_b = _dot(ob_ref[...], wb_ref[...])
    y_c = _dot(yc_ref[...], wc_ref[...]) + bc_ref[...]
    g = gate_ref[...].astype(F32)
    merged = g[:, :D_MODEL] * y_a + g[:, D_MODEL:2 * D_MODEL] * y_b + g[:, 2 * D_MODEL:] * y_c
    o_ref[...] = x_ref[...] + _dot(merged.astype(BF16), wo_ref[...])


def _outproj(x, ya, ob, yc, gates, w):
    tok = x.shape[0]
    tile = lambda n: pl.BlockSpec((TM_OUT, n), lambda i: (i, 0))
    consts = [w["lru_w_out"], w["mla_w_o"], w["conv_w_out"], w["conv_b_out"], w["w_out"]]
    return pl.pallas_call(
        _outproj_kernel,
        out_shape=jax.ShapeDtypeStruct((tok, D_MODEL), F32),
        grid=(tok // TM_OUT,),
        in_specs=[tile(D_MODEL), tile(LRU_WIDTH), tile(MLA_HEADS * V_HEAD_DIM), tile(CONV_CH), tile(IN_G)]
        + [_const_spec(c.shape) for c in consts],
        out_specs=tile(D_MODEL),
        compiler_params=_params(1),
        name="outproj",
    )(x, ya, ob, yc, gates, *consts)


def _prep_layer(p):
    row = lambda v: v.reshape(1, -1).astype(F32)
    o1, o2, o3 = IN_A, IN_A + IN_B, IN_A + IN_B + IN_C
    w_in, b_in = p["w_in"], p["b_in"]
    oq, okv, ope = o1, o1 + Q_LORA_RANK, o1 + Q_LORA_RANK + KV_LORA_RANK
    half = QK_ROPE_DIM // 2

    def rope_cols(m):
        z = lambda n: jnp.zeros(m.shape[:-1] + (n,), m.dtype)
        plain = jnp.concatenate([z(QK_NOPE_DIM), m, z(HEAD_PAD - QK_DIM)], axis=-1)
        rot = jnp.concatenate([z(QK_NOPE_DIM), -m[..., half:], m[..., :half], z(HEAD_PAD - QK_DIM)], axis=-1)
        return plain, rot

    wpe_plain, wpe_rot = rope_cols(w_in[:, ope:o2])
    bpe_plain, bpe_rot = rope_cols(b_in[ope:o2])

    w_uq = p["w_uq"].reshape(Q_LORA_RANK, MLA_HEADS, QK_DIM)
    zq = jnp.zeros((Q_LORA_RANK, MLA_HEADS, HEAD_PAD - QK_DIM), F32)
    q_plain = jnp.concatenate([w_uq, zq], axis=-1)
    q_pe = w_uq[..., QK_NOPE_DIM:]
    q_rot = jnp.concatenate([jnp.zeros((Q_LORA_RANK, MLA_HEADS, QK_NOPE_DIM), F32), -q_pe[..., half:], q_pe[..., :half],
                             zq], axis=-1)
    wuq = jnp.concatenate([q_plain.reshape(Q_LORA_RANK, -1), q_rot.reshape(Q_LORA_RANK, -1)], axis=-1)

    w_ukv = p["w_ukv"].reshape(KV_LORA_RANK, MLA_HEADS, QK_NOPE_DIM + V_HEAD_DIM)
    wk = jnp.concatenate([w_ukv[..., :QK_NOPE_DIM],
                          jnp.zeros((KV_LORA_RANK, MLA_HEADS, HEAD_PAD - QK_NOPE_DIM), F32)], axis=-1)
    wv = jnp.concatenate([w_ukv[..., QK_NOPE_DIM:],
                          jnp.zeros((KV_LORA_RANK, MLA_HEADS, HEAD_PAD - V_HEAD_DIM), F32)], axis=-1)

    wg4 = p["lru_w_gate"]
    eye = jnp.eye(LRU_HEADS, dtype=F32)
    bd = lambda blk: jnp.einsum("hde,hg->hdge", blk, eye).reshape(LRU_WIDTH, LRU_WIDTH)
    wbd = jnp.concatenate([bd(wg4[..., :LRU_HEAD_DIM]), bd(wg4[..., LRU_HEAD_DIM:])], axis=-1)
    bgate = jnp.concatenate([p["lru_b_gate"][:, :LRU_HEAD_DIM].reshape(-1),
                             p["lru_b_gate"][:, LRU_HEAD_DIM:].reshape(-1)])

    inv_freq = ROPE_THETA ** (-jnp.arange(0, QK_ROPE_DIM, 2, dtype=F32) / QK_ROPE_DIM)
    invf = jnp.concatenate([jnp.zeros((QK_NOPE_DIM,), F32), inv_freq, inv_freq,
                            jnp.zeros((HEAD_PAD - QK_DIM,), F32)])

    return dict(
        ffn1_norm=row(p["ffn1_norm"]), ffn1_w1=p["ffn1_w1"].astype(BF16), ffn1_w2=p["ffn1_w2"].astype(BF16),
        ffn2_norm=row(p["ffn2_norm"]), ffn2_w1=p["ffn2_w1"].astype(BF16), ffn2_w2=p["ffn2_w2"].astype(BF16),
        mix_norm=row(p["mix_norm"]),
        wa=w_in[:, :o1].astype(BF16), ba=row(b_in[:o1]),
        wq=w_in[:, oq:okv].astype(BF16), bq=row(b_in[oq:okv]),
        wkv=w_in[:, okv:ope].astype(BF16), bkv=row(b_in[okv:ope]),
        wpe=jnp.concatenate([wpe_plain, wpe_rot], axis=-1).astype(BF16),
        bpe=row(jnp.concatenate([bpe_plain, bpe_rot])),
        wc=w_in[:, o2:o3].astype(BF16), bc=row(b_in[o2:o3]),
        wg=w_in[:, o3:].astype(BF16), bg=row(b_in[o3:]),
        q_norm=row(p["q_norm"]), wuq=wuq.astype(BF16), kv_norm=row(p["kv_norm"]),
        wk=wk.reshape(KV_LORA_RANK, -1).astype(BF16), wv=wv.reshape(KV_LORA_RANK, -1).astype(BF16),
        invf=row(invf),
        lru_conv_w=p["lru_conv_w"].astype(F32), lru_conv_b=row(p["lru_conv_b"]),
        wbd=wbd.astype(BF16), bgate=row(bgate), lam=row(p["lru_lambda"]),
        conv_dw_w=p["conv_dw_w"].astype(F32), conv_dw_b=row(p["conv_dw_b"]),
        conv_ln_g=row(p["conv_ln_g"]), conv_ln_b=row(p["conv_ln_b"]),
        lru_w_out=p["lru_w_out"].astype(BF16), mla_w_o=p["mla_w_o"].astype(BF16),
        conv_w_out=p["conv_w_out"].astype(BF16), conv_b_out=row(p["conv_b_out"]), w_out=p["w_out"].astype(BF16),
    )


def kernel(x, positions, ffn1_norm, ffn1_w1, ffn1_w2, mix_norm, w_in, b_in, lru_conv_w, lru_conv_b, lru_w_gate, lru_b_gate, lru_lambda, lru_w_out, q_norm, w_uq, kv_norm, w_ukv, mla_w_o, conv_dw_w, conv_dw_b, conv_ln_g, conv_ln_b, conv_w_out, conv_b_out, w_out, ffn2_norm, ffn2_w1, ffn2_w2, final_norm):
    stacked = dict(ffn1_norm=ffn1_norm, ffn1_w1=ffn1_w1, ffn1_w2=ffn1_w2, mix_norm=mix_norm, w_in=w_in, b_in=b_in,
                   lru_conv_w=lru_conv_w, lru_conv_b=lru_conv_b, lru_w_gate=lru_w_gate, lru_b_gate=lru_b_gate,
                   lru_lambda=lru_lambda, lru_w_out=lru_w_out, q_norm=q_norm, w_uq=w_uq, kv_norm=kv_norm,
                   w_ukv=w_ukv, mla_w_o=mla_w_o, conv_dw_w=conv_dw_w, conv_dw_b=conv_dw_b, conv_ln_g=conv_ln_g,
                   conv_ln_b=conv_ln_b, conv_w_out=conv_w_out, conv_b_out=conv_b_out, w_out=w_out,
                   ffn2_norm=ffn2_norm, ffn2_w1=ffn2_w1, ffn2_w2=ffn2_w2)
    tok = BATCH * SEQ
    pos = positions.reshape(BATCH, SEQ, 1)
    final_g = final_norm.reshape(1, D_MODEL).astype(F32)
    xf = x.reshape(tok, D_MODEL)
    for l in range(DEPTH):
        w = _prep_layer({k: v[l] for k, v in stacked.items()})
        xf = _ffn(xf, w["ffn1_norm"], w["ffn1_w1"], w["ffn1_w2"])
        xa, ga, q, k, v, c, gates = _inproj(xf.reshape(BATCH, SEQ, D_MODEL), pos, w)
        ya, yc = _seqmix(xa, ga, c, w)
        ob = _attention(q, k, v)
        xf = _outproj(xf, ya.reshape(tok, LRU_WIDTH), ob.reshape(tok, MLA_HEADS * V_HEAD_DIM),
                      yc.reshape(tok, CONV_CH), gates.reshape(tok, IN_G), w)
        xf = _ffn(xf, w["ffn2_norm"], w["ffn2_w1"], w["ffn2_w2"], final_g if l == DEPTH - 1 else None)
    return xf.reshape(BATCH, SEQ, D_MODEL)
```

```python
import functools

import jax
import jax.numpy as jnp
from jax import lax
from jax.experimental import pallas as pl
from jax.experimental.pallas import tpu as pltpu

D_MODEL = 1024
BATCH = 8
SEQ = 2048
DEPTH = 2
D_FF = 2816
NORM_EPS = 1e-6
LRU_WIDTH = 512
LRU_HEADS = 8
LRU_HEAD_DIM = LRU_WIDTH // LRU_HEADS
LRU_CONV_WIDTH = 4
LRU_C = 8.0
MLA_HEADS = 8
QK_NOPE_DIM = 64
QK_ROPE_DIM = 32
V_HEAD_DIM = 64
Q_LORA_RANK = 384
KV_LORA_RANK = 256
ROPE_THETA = 10000.0
CONV_CH = 512
CONV_WIDTH = 31
N_BRANCH = 3
IN_A = 2 * LRU_WIDTH
IN_B = Q_LORA_RANK + KV_LORA_RANK + QK_ROPE_DIM
IN_C = 2 * CONV_CH
IN_G = N_BRANCH * D_MODEL

LANES = 128
SUBLANES = 8
HEAD_PAD = LANES
QK_DIM = QK_NOPE_DIM + QK_ROPE_DIM
LOG2_E = 1.4426950408889634
VMEM_LIMIT = 56 * 1024 * 1024

TM_FFN = 512
FF_CHUNK = 256
TM_IN = 256
TM_OUT = 512
CONV_PAD = 32
LRU_PAD = SUBLANES
TQ = 256
TK = 256

F32 = jnp.float32
BF16 = jnp.bfloat16


def _const_spec(shape):
    nd = len(shape)
    return pl.BlockSpec(shape, lambda *_: (0,) * nd, pipeline_mode=pl.Buffered(1))


def _params(*semantics):
    return pltpu.CompilerParams(dimension_semantics=semantics, vmem_limit_bytes=VMEM_LIMIT)


def _rms(x, g):
    return x * lax.rsqrt(jnp.mean(x * x, axis=-1, keepdims=True) + NORM_EPS) * g


def _dot(a, b):
    return jnp.dot(a, b, preferred_element_type=F32)


def _ffn_kernel(x_ref, g_ref, w1_ref, w2_ref, *rest, final):
    if final:
        fg_ref, o_ref = rest
    else:
        (o_ref,) = rest
    x = x_ref[...]
    xn = _rms(x, g_ref[...]).astype(BF16)
    acc = None
    for c in range(D_FF // FF_CHUNK):
        lo = c * FF_CHUNK
        g = _dot(xn, w1_ref[:, lo:lo + FF_CHUNK])
        u = _dot(xn, w1_ref[:, D_FF + lo:D_FF + lo + FF_CHUNK])
        h = (jax.nn.silu(g) * u).astype(BF16)
        d = _dot(h, w2_ref[lo:lo + FF_CHUNK, :])
        acc = d if acc is None else acc + d
    y = x + 0.5 * acc
    if final:
        y = _rms(y, fg_ref[...])
    o_ref[...] = y


def _ffn(x, norm_g, w1, w2, final_g=None):
    tok = x.shape[0]
    final = final_g is not None
    in_specs = [pl.BlockSpec((TM_FFN, D_MODEL), lambda i: (i, 0)),
                _const_spec((1, D_MODEL)), _const_spec((D_MODEL, 2 * D_FF)), _const_spec((D_FF, D_MODEL))]
    args = [x, norm_g, w1, w2]
    if final:
        in_specs.append(_const_spec((1, D_MODEL)))
        args.append(final_g)
    return pl.pallas_call(
        functools.partial(_ffn_kernel, final=final),
        out_shape=jax.ShapeDtypeStruct((tok, D_MODEL), F32),
        grid=(tok // TM_FFN,),
        in_specs=in_specs,
        out_specs=pl.BlockSpec((TM_FFN, D_MODEL), lambda i: (i, 0)),
        compiler_params=_params("parallel"),
        name="ffn_final" if final else "ffn",
    )(*args)


def _causal_taps(win, w_ref, n_taps, pad, rows):
    n = win.shape[0]
    offs = [pad - (n_taps - 1) + j for j in range(n_taps)]
    acc = None
    for r in range(SUBLANES):
        taps = [j for j in range(n_taps) if offs[j] % SUBLANES == r]
        if not taps:
            continue
        shifted = win if r == 0 else pltpu.roll(win, n - r, axis=0)
        for j in taps:
            lo = offs[j] - r
            term = w_ref[j:j + 1, :] * shifted[lo:lo + rows, :]
            acc = term if acc is None else acc + term
    return acc


def _linear_scan(a, u, h0):
    rows, width = a.shape
    row = lax.broadcasted_iota(jnp.int32, (SUBLANES, width), 0)
    keeps = [(d, row >= d) for d in (1, 2, 4)]
    h, out = h0, []
    for k in range(rows // SUBLANES):
        av = a[k * SUBLANES:(k + 1) * SUBLANES, :]
        hv = u[k * SUBLANES:(k + 1) * SUBLANES, :]
        for d, keep in keeps:
            hv = jnp.where(keep, av * pltpu.roll(hv, d, axis=0) + hv, hv)
            av = jnp.where(keep, av * pltpu.roll(av, d, axis=0), av)
        hv = hv + av * h
        out.append(hv)
        h = jnp.broadcast_to(hv[SUBLANES - 1:SUBLANES, :], (SUBLANES, width))
    return jnp.concatenate(out, axis=0), h


def _inproj_kernel(x_ref, pos_ref, ng_ref, wa_ref, ba_ref, wq_ref, bq_ref, wkv_ref, bkv_ref, wpe_ref, bpe_ref,
                   wc_ref, bc_ref, qn_ref, wuq_ref, kvn_ref, wk_ref, wv_ref, invf_ref,
                   cw_ref, cb_ref, wbd_ref, bgate_ref, lam_ref,
                   ya_ref, q_ref, k_ref, v_ref, c_ref, tail_scr, h_scr):
    @pl.when(pl.program_id(1) == 0)
    def _():
        tail_scr[...] = jnp.zeros(tail_scr.shape, F32)
        h_scr[...] = jnp.zeros(h_scr.shape, F32)

    xn = _rms(x_ref[...], ng_ref[...]).astype(BF16)
    pa = _dot(xn, wa_ref[...]) + ba_ref[...]
    xa_pre = pa[:, :LRU_WIDTH]
    win = jnp.concatenate([tail_scr[...], xa_pre], axis=0)
    tail_scr[...] = xa_pre[TM_IN - LRU_PAD:, :]
    xa = cb_ref[...] + _causal_taps(win, cw_ref, LRU_CONV_WIDTH, LRU_PAD, TM_IN)
    gates = _dot(xa.astype(BF16), wbd_ref[...]) + bgate_ref[...]
    r = jax.nn.sigmoid(gates[:, :LRU_WIDTH])
    ig = jax.nn.sigmoid(gates[:, LRU_WIDTH:])
    log_a = (-LRU_C) * r * jax.nn.softplus(-lam_ref[...])
    a = jnp.exp(log_a)
    e = a * a
    y2 = 2.0 * log_a
    one_m = jnp.where(e == 1.0, -y2, (1.0 - e) * y2 / jnp.log(e))
    h, h_scr[...] = _linear_scan(a, jnp.sqrt(one_m) * (ig * xa), h_scr[...])
    ya_ref[...] = (h * jax.nn.gelu(pa[:, LRU_WIDTH:])).astype(BF16)
    pc = _dot(xn, wc_ref[...]) + bc_ref[...]
    c_ref[...] = pc[:, :CONV_CH] * jax.nn.sigmoid(pc[:, CONV_CH:])
    ang = pos_ref[...].astype(F32) * invf_ref[...]
    cosf, sinf = jnp.cos(ang), jnp.sin(ang)
    scale = QK_DIM ** -0.5 * LOG2_E
    cq = _dot(xn, wq_ref[...]) + bq_ref[...]
    qq = _dot(_rms(cq, qn_ref[...]).astype(BF16), wuq_ref[...])
    nq = MLA_HEADS * HEAD_PAD
    for hd in range(MLA_HEADS):
        lo = hd * HEAD_PAD
        qh = qq[:, lo:lo + HEAD_PAD] * cosf + qq[:, nq + lo:nq + lo + HEAD_PAD] * sinf
        q_ref[hd] = (qh * scale).astype(BF16)
    ckv = _dot(xn, wkv_ref[...]) + bkv_ref[...]
    ckvn = _rms(ckv, kvn_ref[...]).astype(BF16)
    kk = _dot(ckvn, wk_ref[...])
    vv = _dot(ckvn, wv_ref[...])
    pe = _dot(xn, wpe_ref[...]) + bpe_ref[...]
    kpe = pe[:, :HEAD_PAD] * cosf + pe[:, HEAD_PAD:] * sinf
    for hd in range(MLA_HEADS):
        k_ref[hd] = (kk[:, hd * HEAD_PAD:(hd + 1) * HEAD_PAD] + kpe).astype(BF16)
    ones_lane = lax.broadcasted_iota(jnp.int32, (TM_IN, HEAD_PAD), 1) == V_HEAD_DIM
    for hd in range(MLA_HEADS):
        v_ref[hd] = jnp.where(ones_lane, 1.0, vv[:, hd * HEAD_PAD:(hd + 1) * HEAD_PAD]).astype(BF16)


def _inproj(x, pos, w):
    nt = SEQ // TM_IN
    tile = lambda n: pl.BlockSpec((None, TM_IN, n), lambda b, i: (b, i, 0))
    heads = lambda n: pl.BlockSpec((None, n, TM_IN, LANES), lambda b, i: (b, 0, i, 0))
    consts = [w["mix_norm"], w["wa"], w["ba"], w["wq"], w["bq"], w["wkv"], w["bkv"], w["wpe"], w["bpe"],
              w["wc"], w["bc"], w["q_norm"], w["wuq"], w["kv_norm"], w["wk"], w["wv"], w["invf"],
              w["lru_conv_w"], w["lru_conv_b"], w["wbd"], w["bgate"], w["lam"]]
    carry = pltpu.VMEM((LRU_PAD, LRU_WIDTH), F32)
    return pl.pallas_call(
        _inproj_kernel,
        out_shape=(jax.ShapeDtypeStruct((BATCH, SEQ, LRU_WIDTH), BF16),
                   jax.ShapeDtypeStruct((BATCH, MLA_HEADS, SEQ, HEAD_PAD), BF16),
                   jax.ShapeDtypeStruct((BATCH, MLA_HEADS, SEQ, HEAD_PAD), BF16),
                   jax.ShapeDtypeStruct((BATCH, MLA_HEADS, SEQ, HEAD_PAD), BF16),
                   jax.ShapeDtypeStruct((BATCH, SEQ, CONV_CH), F32)),
        grid=(BATCH, nt),
        in_specs=[tile(D_MODEL), tile(1)] + [_const_spec(c.shape) for c in consts],
        out_specs=(tile(LRU_WIDTH), heads(MLA_HEADS), heads(MLA_HEADS), heads(MLA_HEADS), tile(CONV_CH)),
        scratch_shapes=[carry, carry],
        compiler_params=_params("parallel", "arbitrary"),
        name="inproj",
    )(x, pos, *consts)


def _attn_kernel(q_ref, k_ref, v_ref, o_ref, s_scr, mx_scr, acc_scr):
    qi = pl.program_id(1)
    groups = TK // LANES
    row = lax.broadcasted_iota(jnp.int32, (TQ, TK), 0)
    col = lax.broadcasted_iota(jnp.int32, (TQ, TK), 1)
    mx_scr[...] = jnp.full(mx_scr.shape, -jnp.inf, F32)

    def scores(j, diagonal):
        k0 = pl.multiple_of(j * TK, TK)
        for h in range(MLA_HEADS):
            s = lax.dot_general(q_ref[h], k_ref[h, pl.ds(k0, TK), :], (((1,), (1,)), ((), ())),
                                preferred_element_type=F32)
            if diagonal:
                s = jnp.where(col <= row, s, -jnp.inf)
            s_scr[h, j] = s
            mx = mx_scr[h]
            for g in range(groups):
                mx = jnp.maximum(mx, s[:, g * LANES:(g + 1) * LANES])
            mx_scr[h] = mx

    def pairwise(n, fn):
        def two(jj, carry):
            fn(2 * jj)
            fn(2 * jj + 1)
            return carry

        lax.fori_loop(0, n // 2, two, 0)

        @pl.when(n % 2 == 1)
        def _():
            fn(n - 1)

    pairwise(qi, lambda j: scores(j, False))
    scores(qi, True)

    for h in range(MLA_HEADS):
        mx_scr[h] = jnp.broadcast_to(jnp.max(mx_scr[h], axis=1, keepdims=True), (TQ, LANES))
    acc_scr[...] = jnp.zeros(acc_scr.shape, F32)

    def accumulate(j):
        k0 = pl.multiple_of(j * TK, TK)
        for h in range(MLA_HEADS):
            p = jnp.exp2(s_scr[h, j] - jnp.concatenate([mx_scr[h]] * groups, axis=1))
            acc_scr[h] += _dot(p.astype(BF16), v_ref[h, pl.ds(k0, TK), :])

    pairwise(qi + 1, accumulate)

    outs = []
    for h in range(MLA_HEADS):
        acc = acc_scr[h]
        outs.append(acc[:, :V_HEAD_DIM] / acc[:, V_HEAD_DIM:V_HEAD_DIM + 1])
    o_ref[...] = jnp.concatenate(outs, axis=1).astype(BF16)


def _attention(q, k, v):
    assert TQ == TK
    stat = pltpu.VMEM((MLA_HEADS, TQ, LANES), F32)
    return pl.pallas_call(
        _attn_kernel,
        out_shape=jax.ShapeDtypeStruct((BATCH, SEQ, MLA_HEADS * V_HEAD_DIM), BF16),
        grid=(BATCH, SEQ // TQ),
        in_specs=[pl.BlockSpec((None, MLA_HEADS, TQ, HEAD_PAD), lambda b, i: (b, 0, i, 0)),
                  pl.BlockSpec((None, MLA_HEADS, SEQ, HEAD_PAD), lambda b, i: (b, 0, 0, 0)),
                  pl.BlockSpec((None, MLA_HEADS, SEQ, HEAD_PAD), lambda b, i: (b, 0, 0, 0))],
        out_specs=pl.BlockSpec((None, TQ, MLA_HEADS * V_HEAD_DIM), lambda b, i: (b, i, 0)),
        scratch_shapes=[pltpu.VMEM((MLA_HEADS, SEQ // TK, TQ, TK), F32), stat, stat],
        compiler_params=_params("parallel", "parallel"),
        name="attention",
    )(q, k, v)


def _outproj_kernel(x_ref, ya_ref, ob_ref, c_ref, halo_ref, ng_ref, wg_ref, bg_ref, dw_ref, db_ref, lng_ref, lnb_ref,
                    wa_ref, wb_ref, wc_ref, bc_ref, wo_ref, o_ref):
    x = x_ref[...]
    xn = _rms(x, ng_ref[...]).astype(BF16)

    def gate(b):
        lo = b * D_MODEL
        return jax.nn.sigmoid(_dot(xn, wg_ref[:, lo:lo + D_MODEL]) + bg_ref[:, lo:lo + D_MODEL])

    merged = gate(0) * _dot(ya_ref[...], wa_ref[...])
    merged = merged + gate(1) * _dot(ob_ref[...], wb_ref[...])
    halo = jnp.where(pl.program_id(1) == 0, 0.0, halo_ref[...])
    win = jnp.concatenate([halo, c_ref[...]], axis=0)
    acc = db_ref[...] + _causal_taps(win, dw_ref, CONV_WIDTH, CONV_PAD, TM_OUT)
    mu = jnp.mean(acc, axis=-1, keepdims=True)
    dlt = acc - mu
    var = jnp.mean(dlt * dlt, axis=-1, keepdims=True)
    yn = dlt * lax.rsqrt(var + NORM_EPS) * lng_ref[...] + lnb_ref[...]
    y_c = _dot(jax.nn.silu(yn).astype(BF16), wc_ref[...]) + bc_ref[...]
    merged = merged + gate(2) * y_c
    o_ref[...] = x + _dot(merged.astype(BF16), wo_ref[...])


def _outproj(x, ya, ob, c, w):
    tile = lambda n: pl.BlockSpec((None, TM_OUT, n), lambda b, i: (b, i, 0))
    halo_blocks = TM_OUT // CONV_PAD
    halo = pl.BlockSpec((None, CONV_PAD, CONV_CH), lambda b, i: (b, jnp.maximum(i * halo_blocks - 1, 0), 0))
    consts = [w["mix_norm"], w["wg"], w["bg"], w["conv_dw_w"], w["conv_dw_b"], w["conv_ln_g"], w["conv_ln_b"],
              w["lru_w_out"], w["mla_w_o"], w["conv_w_out"], w["conv_b_out"], w["w_out"]]
    return pl.pallas_call(
        _outproj_kernel,
        out_shape=jax.ShapeDtypeStruct((BATCH, SEQ, D_MODEL), F32),
        grid=(BATCH, SEQ // TM_OUT),
        in_specs=[tile(D_MODEL), tile(LRU_WIDTH), tile(MLA_HEADS * V_HEAD_DIM), tile(CONV_CH), halo]
        + [_const_spec(c_.shape) for c_ in consts],
        out_specs=tile(D_MODEL),
        compiler_params=_params("parallel", "parallel"),
        name="outproj",
    )(x, ya, ob, c, c, *consts)


def _prep_layer(p):
    row = lambda v: v.reshape(1, -1).astype(F32)
    o1, o2, o3 = IN_A, IN_A + IN_B, IN_A + IN_B + IN_C
    w_in, b_in = p["w_in"], p["b_in"]
    oq, okv, ope = o1, o1 + Q_LORA_RANK, o1 + Q_LORA_RANK + KV_LORA_RANK
    half = QK_ROPE_DIM // 2

    def rope_cols(m):
        z = lambda n: jnp.zeros(m.shape[:-1] + (n,), m.dtype)
        plain = jnp.concatenate([z(QK_NOPE_DIM), m, z(HEAD_PAD - QK_DIM)], axis=-1)
        rot = jnp.concatenate([z(QK_NOPE_DIM), -m[..., half:], m[..., :half], z(HEAD_PAD - QK_DIM)], axis=-1)
        return plain, rot

    wpe_plain, wpe_rot = rope_cols(w_in[:, ope:o2])
    bpe_plain, bpe_rot = rope_cols(b_in[ope:o2])

    w_uq = p["w_uq"].reshape(Q_LORA_RANK, MLA_HEADS, QK_DIM)
    zq = jnp.zeros((Q_LORA_RANK, MLA_HEADS, HEAD_PAD - QK_DIM), F32)
    q_plain = jnp.concatenate([w_uq, zq], axis=-1)
    q_pe = w_uq[..., QK_NOPE_DIM:]
    q_rot = jnp.concatenate([jnp.zeros((Q_LORA_RANK, MLA_HEADS, QK_NOPE_DIM), F32), -q_pe[..., half:], q_pe[..., :half],
                             zq], axis=-1)
    wuq = jnp.concatenate([q_plain.reshape(Q_LORA_RANK, -1), q_rot.reshape(Q_LORA_RANK, -1)], axis=-1)

    w_ukv = p["w_ukv"].reshape(KV_LORA_RANK, MLA_HEADS, QK_NOPE_DIM + V_HEAD_DIM)
    wk = jnp.concatenate([w_ukv[..., :QK_NOPE_DIM],
                          jnp.zeros((KV_LORA_RANK, MLA_HEADS, HEAD_PAD - QK_NOPE_DIM), F32)], axis=-1)
    wv = jnp.concatenate([w_ukv[..., QK_NOPE_DIM:],
                          jnp.zeros((KV_LORA_RANK, MLA_HEADS, HEAD_PAD - V_HEAD_DIM), F32)], axis=-1)

    wg4 = p["lru_w_gate"]
    eye = jnp.eye(LRU_HEADS, dtype=F32)
    bd = lambda blk: jnp.einsum("hde,hg->hdge", blk, eye).reshape(LRU_WIDTH, LRU_WIDTH)
    wbd = jnp.concatenate([bd(wg4[..., :LRU_HEAD_DIM]), bd(wg4[..., LRU_HEAD_DIM:])], axis=-1)
    bgate = jnp.concatenate([p["lru_b_gate"][:, :LRU_HEAD_DIM].reshape(-1),
                             p["lru_b_gate"][:, LRU_HEAD_DIM:].reshape(-1)])

    inv_freq = ROPE_THETA ** (-jnp.arange(0, QK_ROPE_DIM, 2, dtype=F32) / QK_ROPE_DIM)
    invf = jnp.concatenate([jnp.zeros((QK_NOPE_DIM,), F32), inv_freq, inv_freq,
                            jnp.zeros((HEAD_PAD - QK_DIM,), F32)])

    return dict(
        ffn1_norm=row(p["ffn1_norm"]), ffn1_w1=p["ffn1_w1"].astype(BF16), ffn1_w2=p["ffn1_w2"].astype(BF16),
        ffn2_norm=row(p["ffn2_norm"]), ffn2_w1=p["ffn2_w1"].astype(BF16), ffn2_w2=p["ffn2_w2"].astype(BF16),
        mix_norm=row(p["mix_norm"]),
        wa=w_in[:, :o1].astype(BF16), ba=row(b_in[:o1]),
        wq=w_in[:, oq:okv].astype(BF16), bq=row(b_in[oq:okv]),
        wkv=w_in[:, okv:ope].astype(BF16), bkv=row(b_in[okv:ope]),
        wpe=jnp.concatenate([wpe_plain, wpe_rot], axis=-1).astype(BF16),
        bpe=row(jnp.concatenate([bpe_plain, bpe_rot])),
        wc=w_in[:, o2:o3].astype(BF16), bc=row(b_in[o2:o3]),
        wg=w_in[:, o3:].astype(BF16), bg=row(b_in[o3:]),
        q_norm=row(p["q_norm"]), wuq=wuq.astype(BF16), kv_norm=row(p["kv_norm"]),
        wk=wk.reshape(KV_LORA_RANK, -1).astype(BF16), wv=wv.reshape(KV_LORA_RANK, -1).astype(BF16),
        invf=row(invf),
        lru_conv_w=p["lru_conv_w"].astype(F32), lru_conv_b=row(p["lru_conv_b"]),
        wbd=wbd.astype(BF16), bgate=row(bgate), lam=row(p["lru_lambda"]),
        conv_dw_w=p["conv_dw_w"].astype(F32), conv_dw_b=row(p["conv_dw_b"]),
        conv_ln_g=row(p["conv_ln_g"]), conv_ln_b=row(p["conv_ln_b"]),
        lru_w_out=p["lru_w_out"].astype(BF16), mla_w_o=p["mla_w_o"].astype(BF16),
        conv_w_out=p["conv_w_out"].astype(BF16), conv_b_out=row(p["conv_b_out"]), w_out=p["w_out"].astype(BF16),
    )


def kernel(x, positions, ffn1_norm, ffn1_w1, ffn1_w2, mix_norm, w_in, b_in, lru_conv_w, lru_conv_b, lru_w_gate, lru_b_gate, lru_lambda, lru_w_out, q_norm, w_uq, kv_norm, w_ukv, mla_w_o, conv_dw_w, conv_dw_b, conv_ln_g, conv_ln_b, conv_w_out, conv_b_out, w_out, ffn2_norm, ffn2_w1, ffn2_w2, final_norm):
    stacked = dict(ffn1_norm=ffn1_norm, ffn1_w1=ffn1_w1, ffn1_w2=ffn1_w2, mix_norm=mix_norm, w_in=w_in, b_in=b_in,
                   lru_conv_w=lru_conv_w, lru_conv_b=lru_conv_b, lru_w_gate=lru_w_gate, lru_b_gate=lru_b_gate,
                   lru_lambda=lru_lambda, lru_w_out=lru_w_out, q_norm=q_norm, w_uq=w_uq, kv_norm=kv_norm,
                   w_ukv=w_ukv, mla_w_o=mla_w_o, conv_dw_w=conv_dw_w, conv_dw_b=conv_dw_b, conv_ln_g=conv_ln_g,
                   conv_ln_b=conv_ln_b, conv_w_out=conv_w_out, conv_b_out=conv_b_out, w_out=w_out,
                   ffn2_norm=ffn2_norm, ffn2_w1=ffn2_w1, ffn2_w2=ffn2_w2)
    tok = BATCH * SEQ
    pos = positions.reshape(BATCH, SEQ, 1)
    final_g = final_norm.reshape(1, D_MODEL).astype(F32)
    xf = x.reshape(tok, D_MODEL)
    for l in range(DEPTH):
        w = _prep_layer({k: v[l] for k, v in stacked.items()})
        xf = _ffn(xf, w["ffn1_norm"], w["ffn1_w1"], w["ffn1_w2"])
        xb = xf.reshape(BATCH, SEQ, D_MODEL)
        ya, q, k, v, c = _inproj(xb, pos, w)
        ob = _attention(q, k, v)
        xf = _outproj(xb, ya, ob, c, w).reshape(tok, D_MODEL)
        xf = _ffn(xf, w["ffn2_norm"], w["ffn2_w1"], w["ffn2_w2"], final_g if l == DEPTH - 1 else None)
    return xf.reshape(BATCH, SEQ, D_MODEL)
```

```python
import functools

import jax
import jax.numpy as jnp
from jax import lax
from jax.experimental import pallas as pl
from jax.experimental.pallas import tpu as pltpu

D_MODEL = 1024
BATCH = 8
SEQ = 2048
DEPTH = 2
D_FF = 2816
NORM_EPS = 1e-6
LRU_WIDTH = 512
LRU_HEADS = 8
LRU_HEAD_DIM = LRU_WIDTH // LRU_HEADS
LRU_CONV_WIDTH = 4
LRU_C = 8.0
MLA_HEADS = 8
QK_NOPE_DIM = 64
QK_ROPE_DIM = 32
V_HEAD_DIM = 64
Q_LORA_RANK = 384
KV_LORA_RANK = 256
ROPE_THETA = 10000.0
CONV_CH = 512
CONV_WIDTH = 31
N_BRANCH = 3
IN_A = 2 * LRU_WIDTH
IN_B = Q_LORA_RANK + KV_LORA_RANK + QK_ROPE_DIM
IN_C = 2 * CONV_CH
IN_G = N_BRANCH * D_MODEL

LANES = 128
SUBLANES = 8
HEAD_PAD = LANES
QK_DIM = QK_NOPE_DIM + QK_ROPE_DIM
LOG2_E = 1.4426950408889634
VMEM_LIMIT = 56 * 1024 * 1024

TM_FFN = 512
FF_CHUNK = 256
W_STAGE_CHUNKS = 16
TM_IN = 256
TM_OUT = 512
CONV_PAD = 32
LRU_PAD = SUBLANES
TQ = 256
TK = 256

F32 = jnp.float32
BF16 = jnp.bfloat16


def _layer_spec(stacked, layer):
    nd = stacked.ndim - 1
    return pl.BlockSpec((None,) + stacked.shape[1:], lambda *_: (layer,) + (0,) * nd, pipeline_mode=pl.Buffered(1))


def _params(*semantics):
    return pltpu.CompilerParams(dimension_semantics=semantics, vmem_limit_bytes=VMEM_LIMIT)


def _rms(x, g):
    return x * lax.rsqrt(jnp.mean(x * x, axis=-1, keepdims=True) + NORM_EPS) * g


def _dot(a, b):
    return jnp.dot(a, b, preferred_element_type=F32)


def _stage_weight(src_hbm, layer, dst, stage, sem):
    chunk = stage.shape[1]
    n_chunks = dst.shape[0] // chunk

    def copy(c):
        return pltpu.make_async_copy(src_hbm.at[layer, pl.ds(c * chunk, chunk), :], stage.at[c % 2], sem.at[c % 2])

    copy(0).start()
    for c in range(n_chunks):
        if c + 1 < n_chunks:
            copy(c + 1).start()
        copy(c).wait()
        dst[pl.ds(c * chunk, chunk), :] = stage[c % 2].astype(BF16)


def _ffn_kernel(x_ref, g_ref, w1_hbm, w2_hbm, *rest, layer, final):
    if final:
        fg_ref, o_ref, w1_ref, w2_ref, stage1, stage2, sem = rest
    else:
        o_ref, w1_ref, w2_ref, stage1, stage2, sem = rest

    @pl.when(pl.program_id(0) == 0)
    def _():
        _stage_weight(w1_hbm, layer, w1_ref, stage1, sem)
        _stage_weight(w2_hbm, layer, w2_ref, stage2, sem)

    x = x_ref[...]
    xn = _rms(x, g_ref[...]).astype(BF16)
    acc = None
    for c in range(D_FF // FF_CHUNK):
        lo = c * FF_CHUNK
        g = _dot(xn, w1_ref[:, lo:lo + FF_CHUNK])
        u = _dot(xn, w1_ref[:, D_FF + lo:D_FF + lo + FF_CHUNK])
        h = (jax.nn.silu(g) * u).astype(BF16)
        d = _dot(h, w2_ref[lo:lo + FF_CHUNK, :])
        acc = d if acc is None else acc + d
    y = x + 0.5 * acc
    if final:
        y = _rms(y, fg_ref[...])
    o_ref[...] = y


def _ffn(x, norm_g, w1_stack, w2_stack, layer, final_g=None):
    tok = x.shape[0]
    final = final_g is not None
    hbm = pl.BlockSpec(memory_space=pl.ANY)
    in_specs = [pl.BlockSpec((TM_FFN, D_MODEL), lambda i: (i, 0)), _layer_spec(norm_g, layer), hbm, hbm]
    args = [x, norm_g, w1_stack, w2_stack]
    if final:
        in_specs.append(_layer_spec(final_g, 0))
        args.append(final_g)
    return pl.pallas_call(
        functools.partial(_ffn_kernel, layer=layer, final=final),
        out_shape=jax.ShapeDtypeStruct((tok, D_MODEL), F32),
        grid=(tok // TM_FFN,),
        in_specs=in_specs,
        out_specs=pl.BlockSpec((TM_FFN, D_MODEL), lambda i: (i, 0)),
        scratch_shapes=[pltpu.VMEM((D_MODEL, 2 * D_FF), BF16), pltpu.VMEM((D_FF, D_MODEL), BF16),
                        pltpu.VMEM((2, D_MODEL // W_STAGE_CHUNKS, 2 * D_FF), F32),
                        pltpu.VMEM((2, D_FF // W_STAGE_CHUNKS, D_MODEL), F32),
                        pltpu.SemaphoreType.DMA((2,))],
        compiler_params=_params("arbitrary"),
        name="ffn_final" if final else "ffn",
    )(*args)


def _causal_taps(win, w_ref, n_taps, pad, rows):
    n = win.shape[0]
    offs = [pad - (n_taps - 1) + j for j in range(n_taps)]
    acc = None
    for r in range(SUBLANES):
        taps = [j for j in range(n_taps) if offs[j] % SUBLANES == r]
        if not taps:
            continue
        shifted = win if r == 0 else pltpu.roll(win, n - r, axis=0)
        for j in taps:
            lo = offs[j] - r
            term = w_ref[j:j + 1, :] * shifted[lo:lo + rows, :]
            acc = term if acc is None else acc + term
    return acc


def _linear_scan(a, u, h0):
    rows, width = a.shape
    row = lax.broadcasted_iota(jnp.int32, (SUBLANES, width), 0)
    keeps = [(d, row >= d) for d in (1, 2, 4)]
    h, out = h0, []
    for k in range(rows // SUBLANES):
        av = a[k * SUBLANES:(k + 1) * SUBLANES, :]
        hv = u[k * SUBLANES:(k + 1) * SUBLANES, :]
        for d, keep in keeps:
            hv = jnp.where(keep, av * pltpu.roll(hv, d, axis=0) + hv, hv)
            av = jnp.where(keep, av * pltpu.roll(av, d, axis=0), av)
        hv = hv + av * h
        out.append(hv)
        h = jnp.broadcast_to(hv[SUBLANES - 1:SUBLANES, :], (SUBLANES, width))
    return jnp.concatenate(out, axis=0), h


def _inproj_kernel(x_ref, pos_ref, ng_ref, wa_ref, ba_ref, wq_ref, bq_ref, wkv_ref, bkv_ref, wpe_ref, bpe_ref,
                   wc_ref, bc_ref, qn_ref, wuq_ref, kvn_ref, wk_ref, wv_ref, invf_ref,
                   cw_ref, cb_ref, wbd_ref, bgate_ref, lam_ref,
                   ya_ref, q_ref, k_ref, v_ref, c_ref, tail_scr, h_scr):
    @pl.when(pl.program_id(1) == 0)
    def _():
        tail_scr[...] = jnp.zeros(tail_scr.shape, F32)
        h_scr[...] = jnp.zeros(h_scr.shape, F32)

    xn = _rms(x_ref[...], ng_ref[...]).astype(BF16)
    pa = _dot(xn, wa_ref[...]) + ba_ref[...]
    xa_pre = pa[:, :LRU_WIDTH]
    win = jnp.concatenate([tail_scr[...], xa_pre], axis=0)
    tail_scr[...] = xa_pre[TM_IN - LRU_PAD:, :]
    xa = cb_ref[...] + _causal_taps(win, cw_ref, LRU_CONV_WIDTH, LRU_PAD, TM_IN)
    gates = _dot(xa.astype(BF16), wbd_ref[...]) + bgate_ref[...]
    r = jax.nn.sigmoid(gates[:, :LRU_WIDTH])
    ig = jax.nn.sigmoid(gates[:, LRU_WIDTH:])
    log_a = (-LRU_C) * r * jax.nn.softplus(-lam_ref[...])
    a = jnp.exp(log_a)
    e = a * a
    y2 = 2.0 * log_a
    one_m = jnp.where(e == 1.0, -y2, (1.0 - e) * y2 / jnp.log(e))
    h, h_scr[...] = _linear_scan(a, jnp.sqrt(one_m) * (ig * xa), h_scr[...])
    ya_ref[...] = (h * jax.nn.gelu(pa[:, LRU_WIDTH:])).astype(BF16)
    pc = _dot(xn, wc_ref[...]) + bc_ref[...]
    c_ref[...] = pc[:, :CONV_CH] * jax.nn.sigmoid(pc[:, CONV_CH:])
    ang = pos_ref[...].astype(F32) * invf_ref[...]
    cosf, sinf = jnp.cos(ang), jnp.sin(ang)
    scale = QK_DIM ** -0.5 * LOG2_E
    cq = _dot(xn, wq_ref[...]) + bq_ref[...]
    qq = _dot(_rms(cq, qn_ref[...]).astype(BF16), wuq_ref[...])
    nq = MLA_HEADS * HEAD_PAD
    for hd in range(MLA_HEADS):
        lo = hd * HEAD_PAD
        qh = qq[:, lo:lo + HEAD_PAD] * cosf + qq[:, nq + lo:nq + lo + HEAD_PAD] * sinf
        q_ref[hd] = (qh * scale).astype(BF16)
    ckv = _dot(xn, wkv_ref[...]) + bkv_ref[...]
    ckvn = _rms(ckv, kvn_ref[...]).astype(BF16)
    kk = _dot(ckvn, wk_ref[...])
    vv = _dot(ckvn, wv_ref[...])
    pe = _dot(xn, wpe_ref[...]) + bpe_ref[...]
    kpe = pe[:, :HEAD_PAD] * cosf + pe[:, HEAD_PAD:] * sinf
    for hd in range(MLA_HEADS):
        k_ref[hd] = (kk[:, hd * HEAD_PAD:(hd + 1) * HEAD_PAD] + kpe).astype(BF16)
    ones_lane = lax.broadcasted_iota(jnp.int32, (TM_IN, HEAD_PAD), 1) == V_HEAD_DIM
    for hd in range(MLA_HEADS):
        v_ref[hd] = jnp.where(ones_lane, 1.0, vv[:, hd * HEAD_PAD:(hd + 1) * HEAD_PAD]).astype(BF16)


def _inproj(x, pos, w, layer):
    nt = SEQ // TM_IN
    tile = lambda n: pl.BlockSpec((None, TM_IN, n), lambda b, i: (b, i, 0))
    heads = lambda n: pl.BlockSpec((None, n, TM_IN, LANES), lambda b, i: (b, 0, i, 0))
    consts = [w["mix_norm"], w["wa"], w["ba"], w["wq"], w["bq"], w["wkv"], w["bkv"], w["wpe"], w["bpe"],
              w["wc"], w["bc"], w["q_norm"], w["wuq"], w["kv_norm"], w["wk"], w["wv"], w["invf"],
              w["lru_conv_w"], w["lru_conv_b"], w["wbd"], w["bgate"], w["lam"]]
    carry = pltpu.VMEM((LRU_PAD, LRU_WIDTH), F32)
    return pl.pallas_call(
        _inproj_kernel,
        out_shape=(jax.ShapeDtypeStruct((BATCH, SEQ, LRU_WIDTH), BF16),
                   jax.ShapeDtypeStruct((BATCH, MLA_HEADS, SEQ, HEAD_PAD), BF16),
                   jax.ShapeDtypeStruct((BATCH, MLA_HEADS, SEQ, HEAD_PAD), BF16),
                   jax.ShapeDtypeStruct((BATCH, MLA_HEADS, SEQ, HEAD_PAD), BF16),
                   jax.ShapeDtypeStruct((BATCH, SEQ, CONV_CH), F32)),
        grid=(BATCH, nt),
        in_specs=[tile(D_MODEL), tile(1)] + [_layer_spec(c, layer) for c in consts],
        out_specs=(tile(LRU_WIDTH), heads(MLA_HEADS), heads(MLA_HEADS), heads(MLA_HEADS), tile(CONV_CH)),
        scratch_shapes=[carry, carry],
        compiler_params=_params("parallel", "arbitrary"),
        name="inproj",
    )(x, pos, *consts)


def _attn_kernel(q_ref, k_ref, v_ref, o_ref, s_scr, mx_scr, acc_scr):
    qi = pl.program_id(1)
    groups = TK // LANES
    row = lax.broadcasted_iota(jnp.int32, (TQ, TK), 0)
    col = lax.broadcasted_iota(jnp.int32, (TQ, TK), 1)
    mx_scr[...] = jnp.full(mx_scr.shape, -jnp.inf, F32)

    def scores(j, diagonal):
        k0 = pl.multiple_of(j * TK, TK)
        for h in range(MLA_HEADS):
            s = lax.dot_general(q_ref[h], k_ref[h, pl.ds(k0, TK), :], (((1,), (1,)), ((), ())),
                                preferred_element_type=F32)
            if diagonal:
                s = jnp.where(col <= row, s, -jnp.inf)
            s_scr[h, j] = s
            mx = mx_scr[h]
            for g in range(groups):
                mx = jnp.maximum(mx, s[:, g * LANES:(g + 1) * LANES])
            mx_scr[h] = mx

    def pairwise(n, fn):
        def two(jj, carry):
            fn(2 * jj)
            fn(2 * jj + 1)
            return carry

        lax.fori_loop(0, n // 2, two, 0)

        @pl.when(n % 2 == 1)
        def _():
            fn(n - 1)

    pairwise(qi, lambda j: scores(j, False))
    scores(qi, True)

    for h in range(MLA_HEADS):
        mx_scr[h] = jnp.broadcast_to(jnp.max(mx_scr[h], axis=1, keepdims=True), (TQ, LANES))
    acc_scr[...] = jnp.zeros(acc_scr.shape, F32)

    def accumulate(j):
        k0 = pl.multiple_of(j * TK, TK)
        for h in range(MLA_HEADS):
            p = jnp.exp2(s_scr[h, j] - jnp.concatenate([mx_scr[h]] * groups, axis=1))
            acc_scr[h] += _dot(p.astype(BF16), v_ref[h, pl.ds(k0, TK), :])

    pairwise(qi + 1, accumulate)

    outs = []
    for h in range(MLA_HEADS):
        acc = acc_scr[h]
        outs.append(acc[:, :V_HEAD_DIM] / acc[:, V_HEAD_DIM:V_HEAD_DIM + 1])
    o_ref[...] = jnp.concatenate(outs, axis=1).astype(BF16)


def _attention(q, k, v):
    assert TQ == TK
    stat = pltpu.VMEM((MLA_HEADS, TQ, LANES), F32)
    return pl.pallas_call(
        _attn_kernel,
        out_shape=jax.ShapeDtypeStruct((BATCH, SEQ, MLA_HEADS * V_HEAD_DIM), BF16),
        grid=(BATCH, SEQ // TQ),
        in_specs=[pl.BlockSpec((None, MLA_HEADS, TQ, HEAD_PAD), lambda b, i: (b, 0, i, 0)),
                  pl.BlockSpec((None, MLA_HEADS, SEQ, HEAD_PAD), lambda b, i: (b, 0, 0, 0)),
                  pl.BlockSpec((None, MLA_HEADS, SEQ, HEAD_PAD), lambda b, i: (b, 0, 0, 0))],
        out_specs=pl.BlockSpec((None, TQ, MLA_HEADS * V_HEAD_DIM), lambda b, i: (b, i, 0)),
        scratch_shapes=[pltpu.VMEM((MLA_HEADS, SEQ // TK, TQ, TK), F32), stat, stat],
        compiler_params=_params("parallel", "parallel"),
        name="attention",
    )(q, k, v)


def _outproj_kernel(x_ref, ya_ref, ob_ref, c_ref, halo_ref, ng_ref, wg_ref, bg_ref, dw_ref, db_ref, lng_ref, lnb_ref,
                    wa_ref, wb_ref, wc_ref, bc_ref, wo_ref, o_ref):
    x = x_ref[...]
    xn = _rms(x, ng_ref[...]).astype(BF16)

    def gate(b):
        lo = b * D_MODEL
        return jax.nn.sigmoid(_dot(xn, wg_ref[:, lo:lo + D_MODEL]) + bg_ref[:, lo:lo + D_MODEL])

    merged = gate(0) * _dot(ya_ref[...], wa_ref[...])
    merged = merged + gate(1) * _dot(ob_ref[...], wb_ref[...])
    halo = jnp.where(pl.program_id(1) == 0, 0.0, halo_ref[...])
    win = jnp.concatenate([halo, c_ref[...]], axis=0)
    acc = db_ref[...] + _causal_taps(win, dw_ref, CONV_WIDTH, CONV_PAD, TM_OUT)
    mu = jnp.mean(acc, axis=-1, keepdims=True)
    dlt = acc - mu
    var = jnp.mean(dlt * dlt, axis=-1, keepdims=True)
    yn = dlt * lax.rsqrt(var + NORM_EPS) * lng_ref[...] + lnb_ref[...]
    y_c = _dot(jax.nn.silu(yn).astype(BF16), wc_ref[...]) + bc_ref[...]
    merged = merged + gate(2) * y_c
    o_ref[...] = x + _dot(merged.astype(BF16), wo_ref[...])


def _outproj(x, ya, ob, c, w, layer):
    tile = lambda n: pl.BlockSpec((None, TM_OUT, n), lambda b, i: (b, i, 0))
    halo_blocks = TM_OUT // CONV_PAD
    halo = pl.BlockSpec((None, CONV_PAD, CONV_CH), lambda b, i: (b, jnp.maximum(i * halo_blocks - 1, 0), 0))
    consts = [w["mix_norm"], w["wg"], w["bg"], w["conv_dw_w"], w["conv_dw_b"], w["conv_ln_g"], w["conv_ln_b"],
              w["lru_w_out"], w["mla_w_o"], w["conv_w_out"], w["conv_b_out"], w["w_out"]]
    return pl.pallas_call(
        _outproj_kernel,
        out_shape=jax.ShapeDtypeStruct((BATCH, SEQ, D_MODEL), F32),
        grid=(BATCH, SEQ // TM_OUT),
        in_specs=[tile(D_MODEL), tile(LRU_WIDTH), tile(MLA_HEADS * V_HEAD_DIM), tile(CONV_CH), halo]
        + [_layer_spec(c_, layer) for c_ in consts],
        out_specs=tile(D_MODEL),
        compiler_params=_params("parallel", "parallel"),
        name="outproj",
    )(x, ya, ob, c, c, *consts)


def _prep(p):
    row = lambda v: v.reshape(DEPTH, 1, -1).astype(F32)
    o1, o2, o3 = IN_A, IN_A + IN_B, IN_A + IN_B + IN_C
    w_in, b_in = p["w_in"], p["b_in"]
    oq, okv, ope = o1, o1 + Q_LORA_RANK, o1 + Q_LORA_RANK + KV_LORA_RANK
    half = QK_ROPE_DIM // 2

    def rope_cols(m):
        z = lambda n: jnp.zeros(m.shape[:-1] + (n,), m.dtype)
        plain = jnp.concatenate([z(QK_NOPE_DIM), m, z(HEAD_PAD - QK_DIM)], axis=-1)
        rot = jnp.concatenate([z(QK_NOPE_DIM), -m[..., half:], m[..., :half], z(HEAD_PAD - QK_DIM)], axis=-1)
        return plain, rot

    wpe_plain, wpe_rot = rope_cols(w_in[..., ope:o2])
    bpe_plain, bpe_rot = rope_cols(b_in[..., ope:o2])

    w_uq = p["w_uq"].reshape(DEPTH, Q_LORA_RANK, MLA_HEADS, QK_DIM)
    zq = jnp.zeros((DEPTH, Q_LORA_RANK, MLA_HEADS, HEAD_PAD - QK_DIM), F32)
    q_plain = jnp.concatenate([w_uq, zq], axis=-1)
    q_pe = w_uq[..., QK_NOPE_DIM:]
    q_rot = jnp.concatenate([jnp.zeros((DEPTH, Q_LORA_RANK, MLA_HEADS, QK_NOPE_DIM), F32), -q_pe[..., half:],
                             q_pe[..., :half], zq], axis=-1)
    wuq = jnp.concatenate([q_plain.reshape(DEPTH, Q_LORA_RANK, -1), q_rot.reshape(DEPTH, Q_LORA_RANK, -1)], axis=-1)

    w_ukv = p["w_ukv"].reshape(DEPTH, KV_LORA_RANK, MLA_HEADS, QK_NOPE_DIM + V_HEAD_DIM)
    wk = jnp.concatenate([w_ukv[..., :QK_NOPE_DIM],
                          jnp.zeros((DEPTH, KV_LORA_RANK, MLA_HEADS, HEAD_PAD - QK_NOPE_DIM), F32)], axis=-1)
    wv = jnp.concatenate([w_ukv[..., QK_NOPE_DIM:],
                          jnp.zeros((DEPTH, KV_LORA_RANK, MLA_HEADS, HEAD_PAD - V_HEAD_DIM), F32)], axis=-1)

    wg4 = p["lru_w_gate"]
    eye = jnp.eye(LRU_HEADS, dtype=F32)
    bd = lambda blk: jnp.einsum("lhde,hg->lhdge", blk, eye).reshape(DEPTH, LRU_WIDTH, LRU_WIDTH)
    wbd = jnp.concatenate([bd(wg4[..., :LRU_HEAD_DIM]), bd(wg4[..., LRU_HEAD_DIM:])], axis=-1)
    bgate = jnp.concatenate([p["lru_b_gate"][..., :LRU_HEAD_DIM].reshape(DEPTH, -1),
                             p["lru_b_gate"][..., LRU_HEAD_DIM:].reshape(DEPTH, -1)], axis=-1)

    inv_freq = ROPE_THETA ** (-jnp.arange(0, QK_ROPE_DIM, 2, dtype=F32) / QK_ROPE_DIM)
    invf = jnp.concatenate([jnp.zeros((QK_NOPE_DIM,), F32), inv_freq, inv_freq,
                            jnp.zeros((HEAD_PAD - QK_DIM,), F32)])

    return dict(
        ffn1_norm=row(p["ffn1_norm"]), ffn2_norm=row(p["ffn2_norm"]), mix_norm=row(p["mix_norm"]),
        wa=w_in[..., :o1].astype(BF16), ba=row(b_in[..., :o1]),
        wq=w_in[..., oq:okv].astype(BF16), bq=row(b_in[..., oq:okv]),
        wkv=w_in[..., okv:ope].astype(BF16), bkv=row(b_in[..., okv:ope]),
        wpe=jnp.concatenate([wpe_plain, wpe_rot], axis=-1).astype(BF16),
        bpe=row(jnp.concatenate([bpe_plain, bpe_rot], axis=-1)),
        wc=w_in[..., o2:o3].astype(BF16), bc=row(b_in[..., o2:o3]),
        wg=w_in[..., o3:].astype(BF16), bg=row(b_in[..., o3:]),
        q_norm=row(p["q_norm"]), wuq=wuq.astype(BF16), kv_norm=row(p["kv_norm"]),
        wk=wk.reshape(DEPTH, KV_LORA_RANK, -1).astype(BF16), wv=wv.reshape(DEPTH, KV_LORA_RANK, -1).astype(BF16),
        invf=jnp.broadcast_to(invf, (DEPTH, 1, HEAD_PAD)),
        lru_conv_w=p["lru_conv_w"].astype(F32), lru_conv_b=row(p["lru_conv_b"]),
        wbd=wbd.astype(BF16), bgate=row(bgate), lam=row(p["lru_lambda"]),
        conv_dw_w=p["conv_dw_w"].astype(F32), conv_dw_b=row(p["conv_dw_b"]),
        conv_ln_g=row(p["conv_ln_g"]), conv_ln_b=row(p["conv_ln_b"]),
        lru_w_out=p["lru_w_out"].astype(BF16), mla_w_o=p["mla_w_o"].astype(BF16),
        conv_w_out=p["conv_w_out"].astype(BF16), conv_b_out=row(p["conv_b_out"]), w_out=p["w_out"].astype(BF16),
    )


def kernel(x, positions, ffn1_norm, ffn1_w1, ffn1_w2, mix_norm, w_in, b_in, lru_conv_w, lru_conv_b, lru_w_gate, lru_b_gate, lru_lambda, lru_w_out, q_norm, w_uq, kv_norm, w_ukv, mla_w_o, conv_dw_w, conv_dw_b, conv_ln_g, conv_ln_b, conv_w_out, conv_b_out, w_out, ffn2_norm, ffn2_w1, ffn2_w2, final_norm):
    stacked = dict(ffn1_norm=ffn1_norm, mix_norm=mix_norm, w_in=w_in, b_in=b_in,
                   lru_conv_w=lru_conv_w, lru_conv_b=lru_conv_b, lru_w_gate=lru_w_gate, lru_b_gate=lru_b_gate,
                   lru_lambda=lru_lambda, lru_w_out=lru_w_out, q_norm=q_norm, w_uq=w_uq, kv_norm=kv_norm,
                   w_ukv=w_ukv, mla_w_o=mla_w_o, conv_dw_w=conv_dw_w, conv_dw_b=conv_dw_b, conv_ln_g=conv_ln_g,
                   conv_ln_b=conv_ln_b, conv_w_out=conv_w_out, conv_b_out=conv_b_out, w_out=w_out,
                   ffn2_norm=ffn2_norm)
    tok = BATCH * SEQ
    pos = positions.reshape(BATCH, SEQ, 1)
    final_g = final_norm.reshape(1, 1, D_MODEL).astype(F32)
    xf = x.reshape(tok, D_MODEL)
    w = _prep(stacked)
    for l in range(DEPTH):
        xf = _ffn(xf, w["ffn1_norm"], ffn1_w1, ffn1_w2, l)
        xb = xf.reshape(BATCH, SEQ, D_MODEL)
        ya, q, k, v, c = _inproj(xb, pos, w, l)
        ob = _attention(q, k, v)
        xf = _outproj(xb, ya, ob, c, w, l).reshape(tok, D_MODEL)
        xf = _ffn(xf, w["ffn2_norm"], ffn2_w1, ffn2_w2, l, final_g if l == DEPTH - 1 else None)
    return xf.reshape(BATCH, SEQ, D_MODEL)
```

```python
import functools

import jax
import jax.numpy as jnp
from jax import lax
from jax.experimental import pallas as pl
from jax.experimental.pallas import tpu as pltpu

D_MODEL = 1024
BATCH = 8
SEQ = 2048
DEPTH = 2
D_FF = 2816
NORM_EPS = 1e-6
LRU_WIDTH = 512
LRU_HEADS = 8
LRU_HEAD_DIM = LRU_WIDTH // LRU_HEADS
LRU_CONV_WIDTH = 4
LRU_C = 8.0
MLA_HEADS = 8
QK_NOPE_DIM = 64
QK_ROPE_DIM = 32
V_HEAD_DIM = 64
Q_LORA_RANK = 384
KV_LORA_RANK = 256
ROPE_THETA = 10000.0
CONV_CH = 512
CONV_WIDTH = 31
N_BRANCH = 3
IN_A = 2 * LRU_WIDTH
IN_B = Q_LORA_RANK + KV_LORA_RANK + QK_ROPE_DIM
IN_C = 2 * CONV_CH
IN_G = N_BRANCH * D_MODEL

LANES = 128
SUBLANES = 8
HEAD_PAD = LANES
QK_DIM = QK_NOPE_DIM + QK_ROPE_DIM
LOG2_E = 1.4426950408889634
VMEM_LIMIT = 56 * 1024 * 1024

TM_FFN = 512
FF_CHUNK = 256
W_STAGE_CHUNKS = 8
TM_IN = 512
TM_OUT = 512
CONV_PAD = 32
LRU_PAD = SUBLANES
TQ = 256
TK = 256

F32 = jnp.float32
BF16 = jnp.bfloat16


def _layer_spec(stacked, layer):
    nd = stacked.ndim - 1
    return pl.BlockSpec((None,) + stacked.shape[1:], lambda *_: (layer,) + (0,) * nd, pipeline_mode=pl.Buffered(1))


def _params(*semantics):
    return pltpu.CompilerParams(dimension_semantics=semantics, vmem_limit_bytes=VMEM_LIMIT)


def _rms(x, g):
    return x * lax.rsqrt(jnp.mean(x * x, axis=-1, keepdims=True) + NORM_EPS) * g


def _dot(a, b):
    return jnp.dot(a, b, preferred_element_type=F32)


def _stage_weight(src_hbm, layer, dst, stage, sem):
    chunk = stage.shape[1]
    n_chunks = dst.shape[0] // chunk

    def copy(c):
        return pltpu.make_async_copy(src_hbm.at[layer, pl.ds(c * chunk, chunk), :], stage.at[c % 2], sem.at[c % 2])

    copy(0).start()
    for c in range(n_chunks):
        if c + 1 < n_chunks:
            copy(c + 1).start()
        copy(c).wait()
        dst[pl.ds(c * chunk, chunk), :] = stage[c % 2].astype(BF16)


def _ffn_kernel(x_ref, g_ref, w1_hbm, w2_hbm, *rest, layer, final):
    if final:
        fg_ref, o_ref, w1_ref, w2_ref, stage1, stage2, sem = rest
    else:
        o_ref, w1_ref, w2_ref, stage1, stage2, sem = rest

    @pl.when(pl.program_id(0) == 0)
    def _():
        _stage_weight(w1_hbm, layer, w1_ref, stage1, sem)
        _stage_weight(w2_hbm, layer, w2_ref, stage2, sem)

    x = x_ref[...]
    xn = _rms(x, g_ref[...]).astype(BF16)
    acc = None
    for c in range(D_FF // FF_CHUNK):
        lo = c * FF_CHUNK
        g = _dot(xn, w1_ref[:, lo:lo + FF_CHUNK])
        u = _dot(xn, w1_ref[:, D_FF + lo:D_FF + lo + FF_CHUNK])
        h = (jax.nn.silu(g) * u).astype(BF16)
        d = _dot(h, w2_ref[lo:lo + FF_CHUNK, :])
        acc = d if acc is None else acc + d
    y = x + 0.5 * acc
    if final:
        y = _rms(y, fg_ref[...])
    o_ref[...] = y


def _ffn(x, norm_g, w1_stack, w2_stack, layer, final_g=None):
    tok = x.shape[0]
    final = final_g is not None
    hbm = pl.BlockSpec(memory_space=pl.ANY)
    in_specs = [pl.BlockSpec((TM_FFN, D_MODEL), lambda i: (i, 0)), _layer_spec(norm_g, layer), hbm, hbm]
    args = [x, norm_g, w1_stack, w2_stack]
    if final:
        in_specs.append(_layer_spec(final_g, 0))
        args.append(final_g)
    return pl.pallas_call(
        functools.partial(_ffn_kernel, layer=layer, final=final),
        out_shape=jax.ShapeDtypeStruct((tok, D_MODEL), F32),
        grid=(tok // TM_FFN,),
        in_specs=in_specs,
        out_specs=pl.BlockSpec((TM_FFN, D_MODEL), lambda i: (i, 0)),
        scratch_shapes=[pltpu.VMEM((D_MODEL, 2 * D_FF), BF16), pltpu.VMEM((D_FF, D_MODEL), BF16),
                        pltpu.VMEM((2, D_MODEL // W_STAGE_CHUNKS, 2 * D_FF), F32),
                        pltpu.VMEM((2, D_FF // W_STAGE_CHUNKS, D_MODEL), F32),
                        pltpu.SemaphoreType.DMA((2,))],
        compiler_params=_params("arbitrary"),
        name="ffn_final" if final else "ffn",
    )(*args)


def _causal_taps(win, w_ref, n_taps, pad, rows):
    n = win.shape[0]
    offs = [pad - (n_taps - 1) + j for j in range(n_taps)]
    acc = None
    for r in range(SUBLANES):
        taps = [j for j in range(n_taps) if offs[j] % SUBLANES == r]
        if not taps:
            continue
        shifted = win if r == 0 else pltpu.roll(win, n - r, axis=0)
        for j in taps:
            lo = offs[j] - r
            term = w_ref[j:j + 1, :] * shifted[lo:lo + rows, :]
            acc = term if acc is None else acc + term
    return acc


def _linear_scan(a, u, h0):
    rows, width = a.shape
    row = lax.broadcasted_iota(jnp.int32, (SUBLANES, width), 0)
    keeps = [(d, row >= d) for d in (1, 2, 4)]
    h, out = h0, []
    for k in range(rows // SUBLANES):
        av = a[k * SUBLANES:(k + 1) * SUBLANES, :]
        hv = u[k * SUBLANES:(k + 1) * SUBLANES, :]
        for d, keep in keeps:
            hv = jnp.where(keep, av * pltpu.roll(hv, d, axis=0) + hv, hv)
            av = jnp.where(keep, av * pltpu.roll(av, d, axis=0), av)
        hv = hv + av * h
        out.append(hv)
        h = jnp.broadcast_to(hv[SUBLANES - 1:SUBLANES, :], (SUBLANES, width))
    return jnp.concatenate(out, axis=0), h


def _inproj_kernel(x_ref, cos_ref, sin_ref, ng_ref, wa_ref, ba_ref, wlat_ref, blat_ref, wc_ref, bc_ref,
                   qn_ref, wuq_ref, kvn_ref, wukv_ref, cw_ref, cb_ref, wbd_ref, bgate_ref, lam_ref,
                   ya_ref, q_ref, k_ref, v_ref, c_ref, tail_scr, h_scr):
    @pl.when(pl.program_id(1) == 0)
    def _():
        tail_scr[...] = jnp.zeros(tail_scr.shape, F32)
        h_scr[...] = jnp.zeros(h_scr.shape, F32)

    xn = _rms(x_ref[...], ng_ref[...]).astype(BF16)
    pa = _dot(xn, wa_ref[...]) + ba_ref[...]
    xa_pre = pa[:, :LRU_WIDTH]
    win = jnp.concatenate([tail_scr[...], xa_pre], axis=0)
    tail_scr[...] = xa_pre[TM_IN - LRU_PAD:, :]
    xa = cb_ref[...] + _causal_taps(win, cw_ref, LRU_CONV_WIDTH, LRU_PAD, TM_IN)
    gates = _dot(xa.astype(BF16), wbd_ref[...]) + bgate_ref[...]
    r = jax.nn.sigmoid(gates[:, :LRU_WIDTH])
    ig = jax.nn.sigmoid(gates[:, LRU_WIDTH:])
    log_a = (-LRU_C) * r * jax.nn.softplus(-lam_ref[...])
    a = jnp.exp(log_a)
    e = a * a
    y2 = 2.0 * log_a
    one_m = jnp.where(e == 1.0, -y2, (1.0 - e) * y2 / jnp.log(e))
    h, h_scr[...] = _linear_scan(a, jnp.sqrt(one_m) * (ig * xa), h_scr[...])
    ya_ref[...] = (h * jax.nn.gelu(pa[:, LRU_WIDTH:])).astype(BF16)
    pc = _dot(xn, wc_ref[...]) + bc_ref[...]
    c_ref[...] = pc[:, :CONV_CH] * jax.nn.sigmoid(pc[:, CONV_CH:])
    lat = _dot(xn, wlat_ref[...]) + blat_ref[...]
    okv, ope = Q_LORA_RANK, Q_LORA_RANK + KV_LORA_RANK
    qq = _dot(_rms(lat[:, :okv], qn_ref[...]).astype(BF16), wuq_ref[...])
    kv = _dot(_rms(lat[:, okv:ope], kvn_ref[...]).astype(BF16), wukv_ref[...])
    cosf, sinf = cos_ref[...], sin_ref[...]
    kpe = lat[:, ope:ope + HEAD_PAD] * cosf + lat[:, ope + HEAD_PAD:] * sinf
    nq = MLA_HEADS * HEAD_PAD
    ones_lane = lax.broadcasted_iota(jnp.int32, (TM_IN, HEAD_PAD), 1) == V_HEAD_DIM
    for hd in range(MLA_HEADS):
        lo = hd * HEAD_PAD
        q_ref[hd] = (qq[:, lo:lo + HEAD_PAD] * cosf + qq[:, nq + lo:nq + lo + HEAD_PAD] * sinf).astype(BF16)
        k_ref[hd] = (kv[:, lo:lo + HEAD_PAD] + kpe).astype(BF16)
        v_ref[hd] = jnp.where(ones_lane, 1.0, kv[:, nq + lo:nq + lo + HEAD_PAD]).astype(BF16)


def _rope_kernel(pos_ref, invf_ref, cos_ref, sin_ref):
    ang = pos_ref[...].astype(F32) * invf_ref[...]
    cos_ref[...] = jnp.cos(ang)
    sin_ref[...] = jnp.sin(ang)


def _rope_tables(pos, invf):
    tile = lambda n: pl.BlockSpec((None, TM_IN, n), lambda b, i: (b, i, 0))
    table = jax.ShapeDtypeStruct((BATCH, SEQ, HEAD_PAD), F32)
    return pl.pallas_call(
        _rope_kernel,
        out_shape=(table, table),
        grid=(BATCH, SEQ // TM_IN),
        in_specs=[tile(1), pl.BlockSpec((1, HEAD_PAD), lambda b, i: (0, 0))],
        out_specs=(tile(HEAD_PAD), tile(HEAD_PAD)),
        compiler_params=_params("parallel", "parallel"),
        name="rope_tables",
    )(pos, invf)


def _inproj(x, cos, sin, w, layer):
    nt = SEQ // TM_IN
    tile = lambda n: pl.BlockSpec((None, TM_IN, n), lambda b, i: (b, i, 0))
    heads = lambda n: pl.BlockSpec((None, n, TM_IN, LANES), lambda b, i: (b, 0, i, 0))
    consts = [w["mix_norm"], w["wa"], w["ba"], w["wlat"], w["blat"], w["wc"], w["bc"],
              w["q_norm"], w["wuq"], w["kv_norm"], w["wukv"],
              w["lru_conv_w"], w["lru_conv_b"], w["wbd"], w["bgate"], w["lam"]]
    carry = pltpu.VMEM((LRU_PAD, LRU_WIDTH), F32)
    return pl.pallas_call(
        _inproj_kernel,
        out_shape=(jax.ShapeDtypeStruct((BATCH, SEQ, LRU_WIDTH), BF16),
                   jax.ShapeDtypeStruct((BATCH, MLA_HEADS, SEQ, HEAD_PAD), BF16),
                   jax.ShapeDtypeStruct((BATCH, MLA_HEADS, SEQ, HEAD_PAD), BF16),
                   jax.ShapeDtypeStruct((BATCH, MLA_HEADS, SEQ, HEAD_PAD), BF16),
                   jax.ShapeDtypeStruct((BATCH, SEQ, CONV_CH), F32)),
        grid=(BATCH, nt),
        in_specs=[tile(D_MODEL), tile(HEAD_PAD), tile(HEAD_PAD)] + [_layer_spec(c, layer) for c in consts],
        out_specs=(tile(LRU_WIDTH), heads(MLA_HEADS), heads(MLA_HEADS), heads(MLA_HEADS), tile(CONV_CH)),
        scratch_shapes=[carry, carry],
        compiler_params=_params("parallel", "arbitrary"),
        name="inproj",
    )(x, cos, sin, *consts)


def _attn_kernel(q_ref, k_ref, v_ref, o_ref, s_scr, mx_scr, acc_scr):
    qi = pl.program_id(1)
    groups = TK // LANES
    row = lax.broadcasted_iota(jnp.int32, (TQ, TK), 0)
    col = lax.broadcasted_iota(jnp.int32, (TQ, TK), 1)
    mx_scr[...] = jnp.full(mx_scr.shape, -jnp.inf, F32)

    def scores(j, diagonal):
        k0 = pl.multiple_of(j * TK, TK)
        for h in range(MLA_HEADS):
            s = lax.dot_general(q_ref[h], k_ref[h, pl.ds(k0, TK), :], (((1,), (1,)), ((), ())),
                                preferred_element_type=F32)
            if diagonal:
                s = jnp.where(col <= row, s, -jnp.inf)
            s_scr[h, j] = s
            mx = mx_scr[h]
            for g in range(groups):
                mx = jnp.maximum(mx, s[:, g * LANES:(g + 1) * LANES])
            mx_scr[h] = mx

    def pairwise(n, fn):
        def two(jj, carry):
            fn(2 * jj)
            fn(2 * jj + 1)
            return carry

        lax.fori_loop(0, n // 2, two, 0)

        @pl.when(n % 2 == 1)
        def _():
            fn(n - 1)

    pairwise(qi, lambda j: scores(j, False))
    scores(qi, True)

    for h in range(MLA_HEADS):
        mx_scr[h] = jnp.broadcast_to(jnp.max(mx_scr[h], axis=1, keepdims=True), (TQ, LANES))
    acc_scr[...] = jnp.zeros(acc_scr.shape, F32)

    def accumulate(j):
        k0 = pl.multiple_of(j * TK, TK)
        for h in range(MLA_HEADS):
            p = jnp.exp2(s_scr[h, j] - jnp.concatenate([mx_scr[h]] * groups, axis=1))
            acc_scr[h] += _dot(p.astype(BF16), v_ref[h, pl.ds(k0, TK), :])

    pairwise(qi + 1, accumulate)

    outs = []
    for h in range(MLA_HEADS):
        acc = acc_scr[h]
        outs.append(acc[:, :V_HEAD_DIM] / acc[:, V_HEAD_DIM:V_HEAD_DIM + 1])
    o_ref[...] = jnp.concatenate(outs, axis=1).astype(BF16)


def _attention(q, k, v):
    assert TQ == TK
    stat = pltpu.VMEM((MLA_HEADS, TQ, LANES), F32)
    return pl.pallas_call(
        _attn_kernel,
        out_shape=jax.ShapeDtypeStruct((BATCH, SEQ, MLA_HEADS * V_HEAD_DIM), BF16),
        grid=(BATCH, SEQ // TQ),
        in_specs=[pl.BlockSpec((None, MLA_HEADS, TQ, HEAD_PAD), lambda b, i: (b, 0, i, 0)),
                  pl.BlockSpec((None, MLA_HEADS, SEQ, HEAD_PAD), lambda b, i: (b, 0, 0, 0)),
                  pl.BlockSpec((None, MLA_HEADS, SEQ, HEAD_PAD), lambda b, i: (b, 0, 0, 0))],
        out_specs=pl.BlockSpec((None, TQ, MLA_HEADS * V_HEAD_DIM), lambda b, i: (b, i, 0)),
        scratch_shapes=[pltpu.VMEM((MLA_HEADS, SEQ // TK, TQ, TK), F32), stat, stat],
        compiler_params=_params("parallel", "parallel"),
        name="attention",
    )(q, k, v)


def _outproj_kernel(x_ref, ya_ref, ob_ref, c_ref, halo_ref, ng_ref, wg_ref, bg_ref, dw_ref, db_ref, lng_ref, lnb_ref,
                    wa_ref, wb_ref, wc_ref, bc_ref, wo_ref, o_ref):
    x = x_ref[...]
    xn = _rms(x, ng_ref[...]).astype(BF16)

    def gate(b):
        lo = b * D_MODEL
        return jax.nn.sigmoid(_dot(xn, wg_ref[:, lo:lo + D_MODEL]) + bg_ref[:, lo:lo + D_MODEL])

    merged = gate(0) * _dot(ya_ref[...], wa_ref[...])
    merged = merged + gate(1) * _dot(ob_ref[...], wb_ref[...])
    halo = jnp.where(pl.program_id(1) == 0, 0.0, halo_ref[...])
    win = jnp.concatenate([halo, c_ref[...]], axis=0)
    acc = db_ref[...] + _causal_taps(win, dw_ref, CONV_WIDTH, CONV_PAD, TM_OUT)
    mu = jnp.mean(acc, axis=-1, keepdims=True)
    dlt = acc - mu
    var = jnp.mean(dlt * dlt, axis=-1, keepdims=True)
    yn = dlt * lax.rsqrt(var + NORM_EPS) * lng_ref[...] + lnb_ref[...]
    y_c = _dot(jax.nn.silu(yn).astype(BF16), wc_ref[...]) + bc_ref[...]
    merged = merged + gate(2) * y_c
    o_ref[...] = x + _dot(merged.astype(BF16), wo_ref[...])


def _outproj(x, ya, ob, c, w, layer):
    tile = lambda n: pl.BlockSpec((None, TM_OUT, n), lambda b, i: (b, i, 0))
    halo_blocks = TM_OUT // CONV_PAD
    halo = pl.BlockSpec((None, CONV_PAD, CONV_CH), lambda b, i: (b, jnp.maximum(i * halo_blocks - 1, 0), 0))
    consts = [w["mix_norm"], w["wg"], w["bg"], w["conv_dw_w"], w["conv_dw_b"], w["conv_ln_g"], w["conv_ln_b"],
              w["lru_w_out"], w["mla_w_o"], w["conv_w_out"], w["conv_b_out"], w["w_out"]]
    return pl.pallas_call(
        _outproj_kernel,
        out_shape=jax.ShapeDtypeStruct((BATCH, SEQ, D_MODEL), F32),
        grid=(BATCH, SEQ // TM_OUT),
        in_specs=[tile(D_MODEL), tile(LRU_WIDTH), tile(MLA_HEADS * V_HEAD_DIM), tile(CONV_CH), halo]
        + [_layer_spec(c_, layer) for c_ in consts],
        out_specs=tile(D_MODEL),
        compiler_params=_params("parallel", "parallel"),
        name="outproj",
    )(x, ya, ob, c, c, *consts)


def _prep(p):
    row = lambda v: v.reshape(DEPTH, 1, -1).astype(F32)
    o1, o2, o3 = IN_A, IN_A + IN_B, IN_A + IN_B + IN_C
    w_in, b_in = p["w_in"], p["b_in"]
    oq, okv, ope = o1, o1 + Q_LORA_RANK, o1 + Q_LORA_RANK + KV_LORA_RANK
    half = QK_ROPE_DIM // 2

    def rope_cols(m):
        z = lambda n: jnp.zeros(m.shape[:-1] + (n,), m.dtype)
        plain = jnp.concatenate([z(QK_NOPE_DIM), m, z(HEAD_PAD - QK_DIM)], axis=-1)
        rot = jnp.concatenate([z(QK_NOPE_DIM), -m[..., half:], m[..., :half], z(HEAD_PAD - QK_DIM)], axis=-1)
        return plain, rot

    wpe_plain, wpe_rot = rope_cols(w_in[..., ope:o2])
    bpe_plain, bpe_rot = rope_cols(b_in[..., ope:o2])

    w_uq = p["w_uq"].reshape(DEPTH, Q_LORA_RANK, MLA_HEADS, QK_DIM)
    zq = jnp.zeros((DEPTH, Q_LORA_RANK, MLA_HEADS, HEAD_PAD - QK_DIM), F32)
    q_plain = jnp.concatenate([w_uq, zq], axis=-1)
    q_pe = w_uq[..., QK_NOPE_DIM:]
    q_rot = jnp.concatenate([jnp.zeros((DEPTH, Q_LORA_RANK, MLA_HEADS, QK_NOPE_DIM), F32), -q_pe[..., half:],
                             q_pe[..., :half], zq], axis=-1)
    wuq = jnp.concatenate([q_plain.reshape(DEPTH, Q_LORA_RANK, -1), q_rot.reshape(DEPTH, Q_LORA_RANK, -1)], axis=-1)

    w_ukv = p["w_ukv"].reshape(DEPTH, KV_LORA_RANK, MLA_HEADS, QK_NOPE_DIM + V_HEAD_DIM)
    wk = jnp.concatenate([w_ukv[..., :QK_NOPE_DIM],
                          jnp.zeros((DEPTH, KV_LORA_RANK, MLA_HEADS, HEAD_PAD - QK_NOPE_DIM), F32)], axis=-1)
    wv = jnp.concatenate([w_ukv[..., QK_NOPE_DIM:],
                          jnp.zeros((DEPTH, KV_LORA_RANK, MLA_HEADS, HEAD_PAD - V_HEAD_DIM), F32)], axis=-1)

    wg4 = p["lru_w_gate"]
    eye = jnp.eye(LRU_HEADS, dtype=F32)
    bd = lambda blk: jnp.einsum("lhde,hg->lhdge", blk, eye).reshape(DEPTH, LRU_WIDTH, LRU_WIDTH)
    wbd = jnp.concatenate([bd(wg4[..., :LRU_HEAD_DIM]), bd(wg4[..., LRU_HEAD_DIM:])], axis=-1)
    bgate = jnp.concatenate([p["lru_b_gate"][..., :LRU_HEAD_DIM].reshape(DEPTH, -1),
                             p["lru_b_gate"][..., LRU_HEAD_DIM:].reshape(DEPTH, -1)], axis=-1)

    inv_freq = ROPE_THETA ** (-jnp.arange(0, QK_ROPE_DIM, 2, dtype=F32) / QK_ROPE_DIM)
    invf = jnp.concatenate([jnp.zeros((QK_NOPE_DIM,), F32), inv_freq, inv_freq,
                            jnp.zeros((HEAD_PAD - QK_DIM,), F32)])

    return dict(
        ffn1_norm=row(p["ffn1_norm"]), ffn2_norm=row(p["ffn2_norm"]), mix_norm=row(p["mix_norm"]),
        wa=w_in[..., :o1].astype(BF16), ba=row(b_in[..., :o1]),
        wlat=jnp.concatenate([w_in[..., oq:ope], wpe_plain, wpe_rot], axis=-1).astype(BF16),
        blat=row(jnp.concatenate([b_in[..., oq:ope], bpe_plain, bpe_rot], axis=-1)),
        wc=w_in[..., o2:o3].astype(BF16), bc=row(b_in[..., o2:o3]),
        wg=w_in[..., o3:].astype(BF16), bg=row(b_in[..., o3:]),
        q_norm=row(p["q_norm"]), wuq=(wuq * (QK_DIM ** -0.5 * LOG2_E)).astype(BF16), kv_norm=row(p["kv_norm"]),
        wukv=jnp.concatenate([wk.reshape(DEPTH, KV_LORA_RANK, -1), wv.reshape(DEPTH, KV_LORA_RANK, -1)],
                             axis=-1).astype(BF16),
        invf=invf.reshape(1, HEAD_PAD),
        lru_conv_w=p["lru_conv_w"].astype(F32), lru_conv_b=row(p["lru_conv_b"]),
        wbd=wbd.astype(BF16), bgate=row(bgate), lam=row(p["lru_lambda"]),
        conv_dw_w=p["conv_dw_w"].astype(F32), conv_dw_b=row(p["conv_dw_b"]),
        conv_ln_g=row(p["conv_ln_g"]), conv_ln_b=row(p["conv_ln_b"]),
        lru_w_out=p["lru_w_out"].astype(BF16), mla_w_o=p["mla_w_o"].astype(BF16),
        conv_w_out=p["conv_w_out"].astype(BF16), conv_b_out=row(p["conv_b_out"]), w_out=p["w_out"].astype(BF16),
    )


def kernel(x, positions, ffn1_norm, ffn1_w1, ffn1_w2, mix_norm, w_in, b_in, lru_conv_w, lru_conv_b, lru_w_gate, lru_b_gate, lru_lambda, lru_w_out, q_norm, w_uq, kv_norm, w_ukv, mla_w_o, conv_dw_w, conv_dw_b, conv_ln_g, conv_ln_b, conv_w_out, conv_b_out, w_out, ffn2_norm, ffn2_w1, ffn2_w2, final_norm):
    stacked = dict(ffn1_norm=ffn1_norm, mix_norm=mix_norm, w_in=w_in, b_in=b_in,
                   lru_conv_w=lru_conv_w, lru_conv_b=lru_conv_b, lru_w_gate=lru_w_gate, lru_b_gate=lru_b_gate,
                   lru_lambda=lru_lambda, lru_w_out=lru_w_out, q_norm=q_norm, w_uq=w_uq, kv_norm=kv_norm,
                   w_ukv=w_ukv, mla_w_o=mla_w_o, conv_dw_w=conv_dw_w, conv_dw_b=conv_dw_b, conv_ln_g=conv_ln_g,
                   conv_ln_b=conv_ln_b, conv_w_out=conv_w_out, conv_b_out=conv_b_out, w_out=w_out,
                   ffn2_norm=ffn2_norm)
    tok = BATCH * SEQ
    pos = positions.reshape(BATCH, SEQ, 1)
    final_g = final_norm.reshape(1, 1, D_MODEL).astype(F32)
    xf = x.reshape(tok, D_MODEL)
    w = _prep(stacked)
    cos, sin = _rope_tables(pos, w["invf"])
    for l in range(DEPTH):
        xf = _ffn(xf, w["ffn1_norm"], ffn1_w1, ffn1_w2, l)
        xb = xf.reshape(BATCH, SEQ, D_MODEL)
        ya, q, k, v, c = _inproj(xb, cos, sin, w, l)
        ob = _attention(q, k, v)
        xf = _outproj(xb, ya, ob, c, w, l).reshape(tok, D_MODEL)
        xf = _ffn(xf, w["ffn2_norm"], ffn2_w1, ffn2_w2, l, final_g if l == DEPTH - 1 else None)
    return xf.reshape(BATCH, SEQ, D_MODEL)
```

```python
import functools

import jax
import jax.numpy as jnp
from jax import lax
from jax.experimental import pallas as pl
from jax.experimental.pallas import tpu as pltpu

D_MODEL = 1024
BATCH = 8
SEQ = 2048
DEPTH = 2
D_FF = 2816
NORM_EPS = 1e-6
LRU_WIDTH = 512
LRU_HEADS = 8
LRU_HEAD_DIM = LRU_WIDTH // LRU_HEADS
LRU_CONV_WIDTH = 4
LRU_C = 8.0
MLA_HEADS = 8
QK_NOPE_DIM = 64
QK_ROPE_DIM = 32
V_HEAD_DIM = 64
Q_LORA_RANK = 384
KV_LORA_RANK = 256
ROPE_THETA = 10000.0
CONV_CH = 512
CONV_WIDTH = 31
N_BRANCH = 3
IN_A = 2 * LRU_WIDTH
IN_B = Q_LORA_RANK + KV_LORA_RANK + QK_ROPE_DIM
IN_C = 2 * CONV_CH
IN_G = N_BRANCH * D_MODEL

LANES = 128
SUBLANES = 8
HEAD_PAD = LANES
QK_DIM = QK_NOPE_DIM + QK_ROPE_DIM
LOG2_E = 1.4426950408889634
VMEM_LIMIT = 56 * 1024 * 1024

TM_FFN = 512
FF_CHUNK = 256
W_STAGE_CHUNKS = 8
TM_IN = 512
TM_OUT = 512
CONV_PAD = 32
LRU_PAD = SUBLANES
TQ = 256
TK = 256

F32 = jnp.float32
BF16 = jnp.bfloat16


def _layer_spec(stacked, layer):
    nd = stacked.ndim - 1
    return pl.BlockSpec((None,) + stacked.shape[1:], lambda *_: (layer,) + (0,) * nd, pipeline_mode=pl.Buffered(1))


def _params(*semantics):
    return pltpu.CompilerParams(dimension_semantics=semantics, vmem_limit_bytes=VMEM_LIMIT)


def _rms(x, g):
    return x * lax.rsqrt(jnp.mean(x * x, axis=-1, keepdims=True) + NORM_EPS) * g


def _dot(a, b):
    return jnp.dot(a, b, preferred_element_type=F32)


def _stage_weight(src_hbm, layer, dst, stage, sem):
    chunk = stage.shape[1]
    n_chunks = dst.shape[0] // chunk

    def copy(c):
        return pltpu.make_async_copy(src_hbm.at[layer, pl.ds(c * chunk, chunk), :], stage.at[c % 2], sem.at[c % 2])

    copy(0).start()
    for c in range(n_chunks):
        if c + 1 < n_chunks:
            copy(c + 1).start()
        copy(c).wait()
        dst[pl.ds(c * chunk, chunk), :] = stage[c % 2].astype(BF16)


def _ffn_kernel(x_ref, g_ref, w1_hbm, w2_hbm, *rest, layer, final):
    if final:
        fg_ref, o_ref, w1_ref, w2_ref, stage1, stage2, sem = rest
    else:
        o_ref, w1_ref, w2_ref, stage1, stage2, sem = rest

    @pl.when(pl.program_id(0) == 0)
    def _():
        _stage_weight(w1_hbm, layer, w1_ref, stage1, sem)
        _stage_weight(w2_hbm, layer, w2_ref, stage2, sem)

    x = x_ref[...]
    xn = _rms(x, g_ref[...]).astype(BF16)
    acc = None
    for c in range(D_FF // FF_CHUNK):
        lo = c * FF_CHUNK
        g = _dot(xn, w1_ref[:, lo:lo + FF_CHUNK])
        u = _dot(xn, w1_ref[:, D_FF + lo:D_FF + lo + FF_CHUNK])
        h = (jax.nn.silu(g) * u).astype(BF16)
        d = _dot(h, w2_ref[lo:lo + FF_CHUNK, :])
        acc = d if acc is None else acc + d
    y = x + 0.5 * acc
    if final:
        y = _rms(y, fg_ref[...])
    o_ref[...] = y


def _ffn(x, norm_g, w1_stack, w2_stack, layer, final_g=None):
    tok = x.shape[0]
    final = final_g is not None
    hbm = pl.BlockSpec(memory_space=pl.ANY)
    in_specs = [pl.BlockSpec((TM_FFN, D_MODEL), lambda i: (i, 0)), _layer_spec(norm_g, layer), hbm, hbm]
    args = [x, norm_g, w1_stack, w2_stack]
    if final:
        in_specs.append(_layer_spec(final_g, 0))
        args.append(final_g)
    return pl.pallas_call(
        functools.partial(_ffn_kernel, layer=layer, final=final),
        out_shape=jax.ShapeDtypeStruct((tok, D_MODEL), F32),
        grid=(tok // TM_FFN,),
        in_specs=in_specs,
        out_specs=pl.BlockSpec((TM_FFN, D_MODEL), lambda i: (i, 0)),
        scratch_shapes=[pltpu.VMEM((D_MODEL, 2 * D_FF), BF16), pltpu.VMEM((D_FF, D_MODEL), BF16),
                        pltpu.VMEM((2, D_MODEL // W_STAGE_CHUNKS, 2 * D_FF), F32),
                        pltpu.VMEM((2, D_FF // W_STAGE_CHUNKS, D_MODEL), F32),
                        pltpu.SemaphoreType.DMA((2,))],
        compiler_params=_params("arbitrary"),
        name="ffn_final" if final else "ffn",
    )(*args)


def _causal_taps(win, w_ref, n_taps, pad, rows):
    n = win.shape[0]
    offs = [pad - (n_taps - 1) + j for j in range(n_taps)]
    acc = None
    for r in range(SUBLANES):
        taps = [j for j in range(n_taps) if offs[j] % SUBLANES == r]
        if not taps:
            continue
        shifted = win if r == 0 else pltpu.roll(win, n - r, axis=0)
        for j in taps:
            lo = offs[j] - r
            term = w_ref[j:j + 1, :] * shifted[lo:lo + rows, :]
            acc = term if acc is None else acc + term
    return acc


def _linear_scan(a, u, h0):
    rows, width = a.shape
    row = lax.broadcasted_iota(jnp.int32, (SUBLANES, width), 0)
    keeps = [(d, row >= d) for d in (1, 2, 4)]
    h, out = h0, []
    for k in range(rows // SUBLANES):
        av = a[k * SUBLANES:(k + 1) * SUBLANES, :]
        hv = u[k * SUBLANES:(k + 1) * SUBLANES, :]
        for d, keep in keeps:
            hv = jnp.where(keep, av * pltpu.roll(hv, d, axis=0) + hv, hv)
            av = jnp.where(keep, av * pltpu.roll(av, d, axis=0), av)
        hv = hv + av * h
        out.append(hv)
        h = jnp.broadcast_to(hv[SUBLANES - 1:SUBLANES, :], (SUBLANES, width))
    return jnp.concatenate(out, axis=0), h


def _inproj_kernel(x_ref, cos_ref, sin_ref, ng_ref, wa_ref, ba_ref, wlat_ref, blat_ref, wc_ref, bc_ref,
                   qn_ref, wuq_ref, kvn_ref, wukv_ref, cw_ref, cb_ref, wbd_ref, bgate_ref, lam_ref,
                   ya_ref, q_ref, k_ref, v_ref, c_ref, tail_scr, h_scr):
    @pl.when(pl.program_id(1) == 0)
    def _():
        tail_scr[...] = jnp.zeros(tail_scr.shape, F32)
        h_scr[...] = jnp.zeros(h_scr.shape, F32)

    xn = _rms(x_ref[...], ng_ref[...]).astype(BF16)
    pa = _dot(xn, wa_ref[...]) + ba_ref[...]
    xa_pre = pa[:, :LRU_WIDTH]
    win = jnp.concatenate([tail_scr[...], xa_pre], axis=0)
    tail_scr[...] = xa_pre[TM_IN - LRU_PAD:, :]
    xa = cb_ref[...] + _causal_taps(win, cw_ref, LRU_CONV_WIDTH, LRU_PAD, TM_IN)
    gates = _dot(xa.astype(BF16), wbd_ref[...]) + bgate_ref[...]
    r = jax.nn.sigmoid(gates[:, :LRU_WIDTH])
    ig = jax.nn.sigmoid(gates[:, LRU_WIDTH:])
    log_a = (-LRU_C) * r * jax.nn.softplus(-lam_ref[...])
    a = jnp.exp(log_a)
    e = a * a
    y2 = 2.0 * log_a
    one_m = jnp.where(e == 1.0, -y2, (1.0 - e) * y2 / jnp.log(e))
    h, h_scr[...] = _linear_scan(a, jnp.sqrt(one_m) * (ig * xa), h_scr[...])
    ya_ref[...] = (h * jax.nn.gelu(pa[:, LRU_WIDTH:])).astype(BF16)
    pc = _dot(xn, wc_ref[...]) + bc_ref[...]
    c_ref[...] = pc[:, :CONV_CH] * jax.nn.sigmoid(pc[:, CONV_CH:])
    lat = _dot(xn, wlat_ref[...]) + blat_ref[...]
    okv, ope = Q_LORA_RANK, Q_LORA_RANK + KV_LORA_RANK
    qq = _dot(_rms(lat[:, :okv], qn_ref[...]).astype(BF16), wuq_ref[...])
    kv = _dot(_rms(lat[:, okv:ope], kvn_ref[...]).astype(BF16), wukv_ref[...])
    cosf, sinf = cos_ref[...], sin_ref[...]
    kpe = lat[:, ope:ope + HEAD_PAD] * cosf + lat[:, ope + HEAD_PAD:] * sinf
    nq = MLA_HEADS * HEAD_PAD
    ones_lane = lax.broadcasted_iota(jnp.int32, (TM_IN, HEAD_PAD), 1) == V_HEAD_DIM
    for hd in range(MLA_HEADS):
        lo = hd * HEAD_PAD
        q_ref[hd] = (qq[:, lo:lo + HEAD_PAD] * cosf + qq[:, nq + lo:nq + lo + HEAD_PAD] * sinf).astype(BF16)
        k_ref[hd] = (kv[:, lo:lo + HEAD_PAD] + kpe).astype(BF16)
        v_ref[hd] = jnp.where(ones_lane, 1.0, kv[:, nq + lo:nq + lo + HEAD_PAD]).astype(BF16)


def _rope_kernel(pos_ref, invf_ref, cos_ref, sin_ref):
    ang = pos_ref[...].astype(F32) * invf_ref[...]
    cos_ref[...] = jnp.cos(ang)
    sin_ref[...] = jnp.sin(ang)


def _rope_tables(pos, invf):
    tile = lambda n: pl.BlockSpec((None, TM_IN, n), lambda b, i: (b, i, 0))
    table = jax.ShapeDtypeStruct((BATCH, SEQ, HEAD_PAD), F32)
    return pl.pallas_call(
        _rope_kernel,
        out_shape=(table, table),
        grid=(BATCH, SEQ // TM_IN),
        in_specs=[tile(1), pl.BlockSpec((1, HEAD_PAD), lambda b, i: (0, 0))],
        out_specs=(tile(HEAD_PAD), tile(HEAD_PAD)),
        compiler_params=_params("parallel", "parallel"),
        name="rope_tables",
    )(pos, invf)


def _inproj(x, cos, sin, w, layer):
    nt = SEQ // TM_IN
    tile = lambda n: pl.BlockSpec((None, TM_IN, n), lambda b, i: (b, i, 0))
    heads = lambda n: pl.BlockSpec((None, n, TM_IN, LANES), lambda b, i: (b, 0, i, 0))
    consts = [w["mix_norm"], w["wa"], w["ba"], w["wlat"], w["blat"], w["wc"], w["bc"],
              w["q_norm"], w["wuq"], w["kv_norm"], w["wukv"],
              w["lru_conv_w"], w["lru_conv_b"], w["wbd"], w["bgate"], w["lam"]]
    carry = pltpu.VMEM((LRU_PAD, LRU_WIDTH), F32)
    return pl.pallas_call(
        _inproj_kernel,
        out_shape=(jax.ShapeDtypeStruct((BATCH, SEQ, LRU_WIDTH), BF16),
                   jax.ShapeDtypeStruct((BATCH, MLA_HEADS, SEQ, HEAD_PAD), BF16),
                   jax.ShapeDtypeStruct((BATCH, MLA_HEADS, SEQ, HEAD_PAD), BF16),
                   jax.ShapeDtypeStruct((BATCH, MLA_HEADS, SEQ, HEAD_PAD), BF16),
                   jax.ShapeDtypeStruct((BATCH, SEQ, CONV_CH), F32)),
        grid=(BATCH, nt),
        in_specs=[tile(D_MODEL), tile(HEAD_PAD), tile(HEAD_PAD)] + [_layer_spec(c, layer) for c in consts],
        out_specs=(tile(LRU_WIDTH), heads(MLA_HEADS), heads(MLA_HEADS), heads(MLA_HEADS), tile(CONV_CH)),
        scratch_shapes=[carry, carry],
        compiler_params=_params("parallel", "arbitrary"),
        name="inproj",
    )(x, cos, sin, *consts)


def _attn_kernel(q_ref, k_ref, v_ref, o_ref, s_scr, mx_scr, acc_scr):
    qi = pl.program_id(1)
    groups = TK // LANES
    row = lax.broadcasted_iota(jnp.int32, (TQ, TK), 0)
    col = lax.broadcasted_iota(jnp.int32, (TQ, TK), 1)
    mx_scr[...] = jnp.full(mx_scr.shape, -jnp.inf, F32)

    def scores(j, diagonal):
        k0 = pl.multiple_of(j * TK, TK)
        for h in range(MLA_HEADS):
            s = lax.dot_general(q_ref[h], k_ref[h, pl.ds(k0, TK), :], (((1,), (1,)), ((), ())),
                                preferred_element_type=F32)
            if diagonal:
                s = jnp.where(col <= row, s, -jnp.inf)
            s_scr[h, j] = s
            mx = mx_scr[h]
            for g in range(groups):
                mx = jnp.maximum(mx, s[:, g * LANES:(g + 1) * LANES])
            mx_scr[h] = mx

    def chunkwise(n, fn):
        def four(t, carry):
            for u in range(4):
                fn(4 * t + u)
            return carry

        lax.fori_loop(0, n // 4, four, 0)
        base = (n // 4) * 4

        @pl.when(n % 4 >= 2)
        def _():
            fn(base)
            fn(base + 1)

        @pl.when(n % 2 == 1)
        def _():
            fn(n - 1)

    chunkwise(qi, lambda j: scores(j, False))
    scores(qi, True)

    for h in range(MLA_HEADS):
        mx_scr[h] = jnp.broadcast_to(jnp.max(mx_scr[h], axis=1, keepdims=True), (TQ, LANES))
    acc_scr[...] = jnp.zeros(acc_scr.shape, F32)

    def accumulate(j):
        k0 = pl.multiple_of(j * TK, TK)
        for h in range(MLA_HEADS):
            p = jnp.exp2(s_scr[h, j] - jnp.concatenate([mx_scr[h]] * groups, axis=1))
            acc_scr[h] += _dot(p.astype(BF16), v_ref[h, pl.ds(k0, TK), :])

    chunkwise(qi + 1, accumulate)

    outs = []
    for h in range(MLA_HEADS):
        acc = acc_scr[h]
        outs.append(acc[:, :V_HEAD_DIM] / acc[:, V_HEAD_DIM:V_HEAD_DIM + 1])
    o_ref[...] = jnp.concatenate(outs, axis=1).astype(BF16)


def _attention(q, k, v):
    assert TQ == TK
    stat = pltpu.VMEM((MLA_HEADS, TQ, LANES), F32)
    return pl.pallas_call(
        _attn_kernel,
        out_shape=jax.ShapeDtypeStruct((BATCH, SEQ, MLA_HEADS * V_HEAD_DIM), BF16),
        grid=(BATCH, SEQ // TQ),
        in_specs=[pl.BlockSpec((None, MLA_HEADS, TQ, HEAD_PAD), lambda b, i: (b, 0, i, 0)),
                  pl.BlockSpec((None, MLA_HEADS, SEQ, HEAD_PAD), lambda b, i: (b, 0, 0, 0)),
                  pl.BlockSpec((None, MLA_HEADS, SEQ, HEAD_PAD), lambda b, i: (b, 0, 0, 0))],
        out_specs=pl.BlockSpec((None, TQ, MLA_HEADS * V_HEAD_DIM), lambda b, i: (b, i, 0)),
        scratch_shapes=[pltpu.VMEM((MLA_HEADS, SEQ // TK, TQ, TK), F32), stat, stat],
        compiler_params=_params("parallel", "parallel"),
        name="attention",
    )(q, k, v)


def _outproj_kernel(x_ref, ya_ref, ob_ref, c_ref, halo_ref, ng_ref, wg_ref, bg_ref, dw_ref, db_ref, lng_ref, lnb_ref,
                    wa_ref, wb_ref, wc_ref, bc_ref, wo_ref, o_ref):
    x = x_ref[...]
    xn = _rms(x, ng_ref[...]).astype(BF16)

    def gate(b):
        lo = b * D_MODEL
        return jax.nn.sigmoid(_dot(xn, wg_ref[:, lo:lo + D_MODEL]) + bg_ref[:, lo:lo + D_MODEL])

    merged = gate(0) * _dot(ya_ref[...], wa_ref[...])
    merged = merged + gate(1) * _dot(ob_ref[...], wb_ref[...])
    halo = jnp.where(pl.program_id(1) == 0, 0.0, halo_ref[...])
    win = jnp.concatenate([halo, c_ref[...]], axis=0)
    acc = db_ref[...] + _causal_taps(win, dw_ref, CONV_WIDTH, CONV_PAD, TM_OUT)
    mu = jnp.mean(acc, axis=-1, keepdims=True)
    dlt = acc - mu
    var = jnp.mean(dlt * dlt, axis=-1, keepdims=True)
    yn = dlt * lax.rsqrt(var + NORM_EPS) * lng_ref[...] + lnb_ref[...]
    y_c = _dot(jax.nn.silu(yn).astype(BF16), wc_ref[...]) + bc_ref[...]
    merged = merged + gate(2) * y_c
    o_ref[...] = x + _dot(merged.astype(BF16), wo_ref[...])


def _outproj(x, ya, ob, c, w, layer):
    tile = lambda n: pl.BlockSpec((None, TM_OUT, n), lambda b, i: (b, i, 0))
    halo_blocks = TM_OUT // CONV_PAD
    halo = pl.BlockSpec((None, CONV_PAD, CONV_CH), lambda b, i: (b, jnp.maximum(i * halo_blocks - 1, 0), 0))
    consts = [w["mix_norm"], w["wg"], w["bg"], w["conv_dw_w"], w["conv_dw_b"], w["conv_ln_g"], w["conv_ln_b"],
              w["lru_w_out"], w["mla_w_o"], w["conv_w_out"], w["conv_b_out"], w["w_out"]]
    return pl.pallas_call(
        _outproj_kernel,
        out_shape=jax.ShapeDtypeStruct((BATCH, SEQ, D_MODEL), F32),
        grid=(BATCH, SEQ // TM_OUT),
        in_specs=[tile(D_MODEL), tile(LRU_WIDTH), tile(MLA_HEADS * V_HEAD_DIM), tile(CONV_CH), halo]
        + [_layer_spec(c_, layer) for c_ in consts],
        out_specs=tile(D_MODEL),
        compiler_params=_params("parallel", "parallel"),
        name="outproj",
    )(x, ya, ob, c, c, *consts)


def _prep(p):
    row = lambda v: v.reshape(DEPTH, 1, -1).astype(F32)
    o1, o2, o3 = IN_A, IN_A + IN_B, IN_A + IN_B + IN_C
    w_in, b_in = p["w_in"], p["b_in"]
    oq, okv, ope = o1, o1 + Q_LORA_RANK, o1 + Q_LORA_RANK + KV_LORA_RANK
    half = QK_ROPE_DIM // 2

    def rope_cols(m):
        z = lambda n: jnp.zeros(m.shape[:-1] + (n,), m.dtype)
        plain = jnp.concatenate([z(QK_NOPE_DIM), m, z(HEAD_PAD - QK_DIM)], axis=-1)
        rot = jnp.concatenate([z(QK_NOPE_DIM), -m[..., half:], m[..., :half], z(HEAD_PAD - QK_DIM)], axis=-1)
        return plain, rot

    wpe_plain, wpe_rot = rope_cols(w_in[..., ope:o2])
    bpe_plain, bpe_rot = rope_cols(b_in[..., ope:o2])

    w_uq = p["w_uq"].reshape(DEPTH, Q_LORA_RANK, MLA_HEADS, QK_DIM)
    zq = jnp.zeros((DEPTH, Q_LORA_RANK, MLA_HEADS, HEAD_PAD - QK_DIM), F32)
    q_plain = jnp.concatenate([w_uq, zq], axis=-1)
    q_pe = w_uq[..., QK_NOPE_DIM:]
    q_rot = jnp.concatenate([jnp.zeros((DEPTH, Q_LORA_RANK, MLA_HEADS, QK_NOPE_DIM), F32), -q_pe[..., half:],
                             q_pe[..., :half], zq], axis=-1)
    wuq = jnp.concatenate([q_plain.reshape(DEPTH, Q_LORA_RANK, -1), q_rot.reshape(DEPTH, Q_LORA_RANK, -1)], axis=-1)

    w_ukv = p["w_ukv"].reshape(DEPTH, KV_LORA_RANK, MLA_HEADS, QK_NOPE_DIM + V_HEAD_DIM)
    wk = jnp.concatenate([w_ukv[..., :QK_NOPE_DIM],
                          jnp.zeros((DEPTH, KV_LORA_RANK, MLA_HEADS, HEAD_PAD - QK_NOPE_DIM), F32)], axis=-1)
    wv = jnp.concatenate([w_ukv[..., QK_NOPE_DIM:],
                          jnp.zeros((DEPTH, KV_LORA_RANK, MLA_HEADS, HEAD_PAD - V_HEAD_DIM), F32)], axis=-1)

    wg4 = p["lru_w_gate"]
    eye = jnp.eye(LRU_HEADS, dtype=F32)
    bd = lambda blk: jnp.einsum("lhde,hg->lhdge", blk, eye).reshape(DEPTH, LRU_WIDTH, LRU_WIDTH)
    wbd = jnp.concatenate([bd(wg4[..., :LRU_HEAD_DIM]), bd(wg4[..., LRU_HEAD_DIM:])], axis=-1)
    bgate = jnp.concatenate([p["lru_b_gate"][..., :LRU_HEAD_DIM].reshape(DEPTH, -1),
                             p["lru_b_gate"][..., LRU_HEAD_DIM:].reshape(DEPTH, -1)], axis=-1)

    inv_freq = ROPE_THETA ** (-jnp.arange(0, QK_ROPE_DIM, 2, dtype=F32) / QK_ROPE_DIM)
    invf = jnp.concatenate([jnp.zeros((QK_NOPE_DIM,), F32), inv_freq, inv_freq,
                            jnp.zeros((HEAD_PAD - QK_DIM,), F32)])

    return dict(
        ffn1_norm=row(p["ffn1_norm"]), ffn2_norm=row(p["ffn2_norm"]), mix_norm=row(p["mix_norm"]),
        wa=w_in[..., :o1].astype(BF16), ba=row(b_in[..., :o1]),
        wlat=jnp.concatenate([w_in[..., oq:ope], wpe_plain, wpe_rot], axis=-1).astype(BF16),
        blat=row(jnp.concatenate([b_in[..., oq:ope], bpe_plain, bpe_rot], axis=-1)),
        wc=w_in[..., o2:o3].astype(BF16), bc=row(b_in[..., o2:o3]),
        wg=w_in[..., o3:].astype(BF16), bg=row(b_in[..., o3:]),
        q_norm=row(p["q_norm"]), wuq=(wuq * (QK_DIM ** -0.5 * LOG2_E)).astype(BF16), kv_norm=row(p["kv_norm"]),
        wukv=jnp.concatenate([wk.reshape(DEPTH, KV_LORA_RANK, -1), wv.reshape(DEPTH, KV_LORA_RANK, -1)],
                             axis=-1).astype(BF16),
        invf=invf.reshape(1, HEAD_PAD),
        lru_conv_w=p["lru_conv_w"].astype(F32), lru_conv_b=row(p["lru_conv_b"]),
        wbd=wbd.astype(BF16), bgate=row(bgate), lam=row(p["lru_lambda"]),
        conv_dw_w=p["conv_dw_w"].astype(F32), conv_dw_b=row(p["conv_dw_b"]),
        conv_ln_g=row(p["conv_ln_g"]), conv_ln_b=row(p["conv_ln_b"]),
        lru_w_out=p["lru_w_out"].astype(BF16), mla_w_o=p["mla_w_o"].astype(BF16),
        conv_w_out=p["conv_w_out"].astype(BF16), conv_b_out=row(p["conv_b_out"]), w_out=p["w_out"].astype(BF16),
    )


def kernel(x, positions, ffn1_norm, ffn1_w1, ffn1_w2, mix_norm, w_in, b_in, lru_conv_w, lru_conv_b, lru_w_gate, lru_b_gate, lru_lambda, lru_w_out, q_norm, w_uq, kv_norm, w_ukv, mla_w_o, conv_dw_w, conv_dw_b, conv_ln_g, conv_ln_b, conv_w_out, conv_b_out, w_out, ffn2_norm, ffn2_w1, ffn2_w2, final_norm):
    stacked = dict(ffn1_norm=ffn1_norm, mix_norm=mix_norm, w_in=w_in, b_in=b_in,
                   lru_conv_w=lru_conv_w, lru_conv_b=lru_conv_b, lru_w_gate=lru_w_gate, lru_b_gate=lru_b_gate,
                   lru_lambda=lru_lambda, lru_w_out=lru_w_out, q_norm=q_norm, w_uq=w_uq, kv_norm=kv_norm,
                   w_ukv=w_ukv, mla_w_o=mla_w_o, conv_dw_w=conv_dw_w, conv_dw_b=conv_dw_b, conv_ln_g=conv_ln_g,
                   conv_ln_b=conv_ln_b, conv_w_out=conv_w_out, conv_b_out=conv_b_out, w_out=w_out,
                   ffn2_norm=ffn2_norm)
    tok = BATCH * SEQ
    pos = positions.reshape(BATCH, SEQ, 1)
    final_g = final_norm.reshape(1, 1, D_MODEL).astype(F32)
    xf = x.reshape(tok, D_MODEL)
    w = _prep(stacked)
    cos, sin = _rope_tables(pos, w["invf"])
    for l in range(DEPTH):
        xf = _ffn(xf, w["ffn1_norm"], ffn1_w1, ffn1_w2, l)
        xb = xf.reshape(BATCH, SEQ, D_MODEL)
        ya, q, k, v, c = _inproj(xb, cos, sin, w, l)
        ob = _attention(q, k, v)
        xf = _outproj(xb, ya, ob, c, w, l).reshape(tok, D_MODEL)
        xf = _ffn(xf, w["ffn2_norm"], ffn2_w1, ffn2_w2, l, final_g if l == DEPTH - 1 else None)
    return xf.reshape(BATCH, SEQ, D_MODEL)
```

```python
import functools

import jax
import jax.numpy as jnp
from jax import lax
from jax.experimental import pallas as pl
from jax.experimental.pallas import tpu as pltpu

D_MODEL = 1024
BATCH = 8
SEQ = 2048
DEPTH = 2
D_FF = 2816
NORM_EPS = 1e-6
LRU_WIDTH = 512
LRU_HEADS = 8
LRU_HEAD_DIM = LRU_WIDTH // LRU_HEADS
LRU_CONV_WIDTH = 4
LRU_C = 8.0
MLA_HEADS = 8
QK_NOPE_DIM = 64
QK_ROPE_DIM = 32
V_HEAD_DIM = 64
Q_LORA_RANK = 384
KV_LORA_RANK = 256
ROPE_THETA = 10000.0
CONV_CH = 512
CONV_WIDTH = 31
N_BRANCH = 3
IN_A = 2 * LRU_WIDTH
IN_B = Q_LORA_RANK + KV_LORA_RANK + QK_ROPE_DIM
IN_C = 2 * CONV_CH
IN_G = N_BRANCH * D_MODEL

LANES = 128
SUBLANES = 8
HEAD_PAD = LANES
QK_DIM = QK_NOPE_DIM + QK_ROPE_DIM
LOG2_E = 1.4426950408889634
VMEM_LIMIT = 56 * 1024 * 1024

TM_FFN = 512
FF_CHUNK = 256
W_STAGE_CHUNKS = 8
TM_IN = 512
TM_OUT = 512
CONV_PAD = 32
LRU_PAD = SUBLANES
TQ = 256
TK = 256

F32 = jnp.float32
BF16 = jnp.bfloat16


def _layer_spec(stacked, layer):
    nd = stacked.ndim - 1
    return pl.BlockSpec((None,) + stacked.shape[1:], lambda *_: (layer,) + (0,) * nd, pipeline_mode=pl.Buffered(1))


def _params(*semantics):
    return pltpu.CompilerParams(dimension_semantics=semantics, vmem_limit_bytes=VMEM_LIMIT)


def _rms(x, g):
    return x * lax.rsqrt(jnp.mean(x * x, axis=-1, keepdims=True) + NORM_EPS) * g


def _dot(a, b):
    return jnp.dot(a, b, preferred_element_type=F32)


def _stage_weight(src_hbm, layer, dst, stage, sem):
    chunk = stage.shape[1]
    n_chunks = dst.shape[0] // chunk

    def copy(c):
        return pltpu.make_async_copy(src_hbm.at[layer, pl.ds(c * chunk, chunk), :], stage.at[c % 2], sem.at[c % 2])

    copy(0).start()
    for c in range(n_chunks):
        if c + 1 < n_chunks:
            copy(c + 1).start()
        copy(c).wait()
        dst[pl.ds(c * chunk, chunk), :] = stage[c % 2].astype(BF16)


def _ffn_kernel(x_ref, g_ref, w1_hbm, w2_hbm, *rest, layer, final):
    if final:
        fg_ref, o_ref, w1_ref, w2_ref, stage1, stage2, sem = rest
    else:
        o_ref, w1_ref, w2_ref, stage1, stage2, sem = rest

    @pl.when(pl.program_id(0) == 0)
    def _():
        _stage_weight(w1_hbm, layer, w1_ref, stage1, sem)
        _stage_weight(w2_hbm, layer, w2_ref, stage2, sem)

    x = x_ref[...]
    xn = _rms(x, g_ref[...]).astype(BF16)
    acc = None
    for c in range(D_FF // FF_CHUNK):
        lo = c * FF_CHUNK
        g = _dot(xn, w1_ref[:, lo:lo + FF_CHUNK])
        u = _dot(xn, w1_ref[:, D_FF + lo:D_FF + lo + FF_CHUNK])
        h = (jax.nn.silu(g) * u).astype(BF16)
        d = _dot(h, w2_ref[lo:lo + FF_CHUNK, :])
        acc = d if acc is None else acc + d
    y = x + 0.5 * acc
    if final:
        y = _rms(y, fg_ref[...])
    o_ref[...] = y


def _ffn(x, norm_g, w1_stack, w2_stack, layer, final_g=None):
    tok = x.shape[0]
    final = final_g is not None
    hbm = pl.BlockSpec(memory_space=pl.ANY)
    in_specs = [pl.BlockSpec((TM_FFN, D_MODEL), lambda i: (i, 0)), _layer_spec(norm_g, layer), hbm, hbm]
    args = [x, norm_g, w1_stack, w2_stack]
    if final:
        in_specs.append(_layer_spec(final_g, 0))
        args.append(final_g)
    return pl.pallas_call(
        functools.partial(_ffn_kernel, layer=layer, final=final),
        out_shape=jax.ShapeDtypeStruct((tok, D_MODEL), F32),
        grid=(tok // TM_FFN,),
        in_specs=in_specs,
        out_specs=pl.BlockSpec((TM_FFN, D_MODEL), lambda i: (i, 0)),
        scratch_shapes=[pltpu.VMEM((D_MODEL, 2 * D_FF), BF16), pltpu.VMEM((D_FF, D_MODEL), BF16),
                        pltpu.VMEM((2, D_MODEL // W_STAGE_CHUNKS, 2 * D_FF), F32),
                        pltpu.VMEM((2, D_FF // W_STAGE_CHUNKS, D_MODEL), F32),
                        pltpu.SemaphoreType.DMA((2,))],
        compiler_params=_params("arbitrary"),
        name="ffn_final" if final else "ffn",
    )(*args)


def _causal_taps(win, w_ref, n_taps, pad, rows):
    n = win.shape[0]
    offs = [pad - (n_taps - 1) + j for j in range(n_taps)]
    acc = None
    for r in range(SUBLANES):
        taps = [j for j in range(n_taps) if offs[j] % SUBLANES == r]
        if not taps:
            continue
        shifted = win if r == 0 else pltpu.roll(win, n - r, axis=0)
        for j in taps:
            lo = offs[j] - r
            term = w_ref[j:j + 1, :] * shifted[lo:lo + rows, :]
            acc = term if acc is None else acc + term
    return acc


def _linear_scan(a, u, h0):
    rows, width = a.shape
    row = lax.broadcasted_iota(jnp.int32, (SUBLANES, width), 0)
    keeps = [(d, row >= d) for d in (1, 2, 4)]
    h, out = h0, []
    for k in range(rows // SUBLANES):
        av = a[k * SUBLANES:(k + 1) * SUBLANES, :]
        hv = u[k * SUBLANES:(k + 1) * SUBLANES, :]
        for d, keep in keeps:
            hv = jnp.where(keep, av * pltpu.roll(hv, d, axis=0) + hv, hv)
            av = jnp.where(keep, av * pltpu.roll(av, d, axis=0), av)
        hv = hv + av * h
        out.append(hv)
        h = jnp.broadcast_to(hv[SUBLANES - 1:SUBLANES, :], (SUBLANES, width))
    return jnp.concatenate(out, axis=0), h


def _inproj_kernel(x_ref, cos_ref, sin_ref, ng_ref, wa_ref, ba_ref, wlat_ref, blat_ref, wc_ref, bc_ref,
                   qn_ref, wuq_ref, kvn_ref, wukv_ref, cw_ref, cb_ref, wbd_ref, bgate_ref, lam_ref,
                   ya_ref, q_ref, k_ref, v_ref, c_ref, tail_scr, h_scr):
    @pl.when(pl.program_id(1) == 0)
    def _():
        tail_scr[...] = jnp.zeros(tail_scr.shape, F32)
        h_scr[...] = jnp.zeros(h_scr.shape, F32)

    xn = _rms(x_ref[...], ng_ref[...]).astype(BF16)
    pa = _dot(xn, wa_ref[...]) + ba_ref[...]
    xa_pre = pa[:, :LRU_WIDTH]
    win = jnp.concatenate([tail_scr[...], xa_pre], axis=0)
    tail_scr[...] = xa_pre[TM_IN - LRU_PAD:, :]
    xa = cb_ref[...] + _causal_taps(win, cw_ref, LRU_CONV_WIDTH, LRU_PAD, TM_IN)
    gates = _dot(xa.astype(BF16), wbd_ref[...]) + bgate_ref[...]
    r = jax.nn.sigmoid(gates[:, :LRU_WIDTH])
    ig = jax.nn.sigmoid(gates[:, LRU_WIDTH:])
    log_a = (-LRU_C) * r * jax.nn.softplus(-lam_ref[...])
    a = jnp.exp(log_a)
    one_m = (1.0 + a * a) * jnp.tanh(-log_a)
    root = jnp.where(one_m > 0.0, one_m * lax.rsqrt(one_m), 0.0)
    h, h_scr[...] = _linear_scan(a, root * (ig * xa), h_scr[...])
    ya_ref[...] = (h * jax.nn.gelu(pa[:, LRU_WIDTH:])).astype(BF16)
    pc = _dot(xn, wc_ref[...]) + bc_ref[...]
    c_ref[...] = pc[:, :CONV_CH] * jax.nn.sigmoid(pc[:, CONV_CH:])
    lat = _dot(xn, wlat_ref[...]) + blat_ref[...]
    okv, ope = Q_LORA_RANK, Q_LORA_RANK + KV_LORA_RANK
    qq = _dot(_rms(lat[:, :okv], qn_ref[...]).astype(BF16), wuq_ref[...])
    kv = _dot(_rms(lat[:, okv:ope], kvn_ref[...]).astype(BF16), wukv_ref[...])
    cosf, sinf = cos_ref[...], sin_ref[...]
    kpe = lat[:, ope:ope + HEAD_PAD] * cosf + lat[:, ope + HEAD_PAD:] * sinf
    nq = MLA_HEADS * HEAD_PAD
    ones_lane = lax.broadcasted_iota(jnp.int32, (TM_IN, HEAD_PAD), 1) == V_HEAD_DIM
    for hd in range(MLA_HEADS):
        lo = hd * HEAD_PAD
        q_ref[hd] = (qq[:, lo:lo + HEAD_PAD] * cosf + qq[:, nq + lo:nq + lo + HEAD_PAD] * sinf).astype(BF16)
        k_ref[hd] = (kv[:, lo:lo + HEAD_PAD] + kpe).astype(BF16)
        v_ref[hd] = jnp.where(ones_lane, 1.0, kv[:, nq + lo:nq + lo + HEAD_PAD]).astype(BF16)


def _rope_kernel(pos_ref, invf_ref, cos_ref, sin_ref):
    ang = pos_ref[...].astype(F32) * invf_ref[...]
    cos_ref[...] = jnp.cos(ang)
    sin_ref[...] = jnp.sin(ang)


def _rope_tables(pos, invf):
    tile = lambda n: pl.BlockSpec((None, TM_IN, n), lambda b, i: (b, i, 0))
    table = jax.ShapeDtypeStruct((BATCH, SEQ, HEAD_PAD), F32)
    return pl.pallas_call(
        _rope_kernel,
        out_shape=(table, table),
        grid=(BATCH, SEQ // TM_IN),
        in_specs=[tile(1), pl.BlockSpec((1, HEAD_PAD), lambda b, i: (0, 0))],
        out_specs=(tile(HEAD_PAD), tile(HEAD_PAD)),
        compiler_params=_params("parallel", "parallel"),
        name="rope_tables",
    )(pos, invf)


def _inproj(x, cos, sin, w, layer):
    nt = SEQ // TM_IN
    tile = lambda n: pl.BlockSpec((None, TM_IN, n), lambda b, i: (b, i, 0))
    heads = lambda n: pl.BlockSpec((None, n, TM_IN, LANES), lambda b, i: (b, 0, i, 0))
    consts = [w["mix_norm"], w["wa"], w["ba"], w["wlat"], w["blat"], w["wc"], w["bc"],
              w["q_norm"], w["wuq"], w["kv_norm"], w["wukv"],
              w["lru_conv_w"], w["lru_conv_b"], w["wbd"], w["bgate"], w["lam"]]
    carry = pltpu.VMEM((LRU_PAD, LRU_WIDTH), F32)
    return pl.pallas_call(
        _inproj_kernel,
        out_shape=(jax.ShapeDtypeStruct((BATCH, SEQ, LRU_WIDTH), BF16),
                   jax.ShapeDtypeStruct((BATCH, MLA_HEADS, SEQ, HEAD_PAD), BF16),
                   jax.ShapeDtypeStruct((BATCH, MLA_HEADS, SEQ, HEAD_PAD), BF16),
                   jax.ShapeDtypeStruct((BATCH, MLA_HEADS, SEQ, HEAD_PAD), BF16),
                   jax.ShapeDtypeStruct((BATCH, SEQ, CONV_CH), F32)),
        grid=(BATCH, nt),
        in_specs=[tile(D_MODEL), tile(HEAD_PAD), tile(HEAD_PAD)] + [_layer_spec(c, layer) for c in consts],
        out_specs=(tile(LRU_WIDTH), heads(MLA_HEADS), heads(MLA_HEADS), heads(MLA_HEADS), tile(CONV_CH)),
        scratch_shapes=[carry, carry],
        compiler_params=_params("parallel", "arbitrary"),
        name="inproj",
    )(x, cos, sin, *consts)


def _attn_kernel(q_ref, k_ref, v_ref, o_ref, s_scr, mx_scr, acc_scr):
    qi = pl.program_id(1)
    groups = TK // LANES
    row = lax.broadcasted_iota(jnp.int32, (TQ, TK), 0)
    col = lax.broadcasted_iota(jnp.int32, (TQ, TK), 1)
    mx_scr[...] = jnp.full(mx_scr.shape, -jnp.inf, F32)

    def scores(j, diagonal):
        k0 = pl.multiple_of(j * TK, TK)
        for h in range(MLA_HEADS):
            s = lax.dot_general(q_ref[h], k_ref[h, pl.ds(k0, TK), :], (((1,), (1,)), ((), ())),
                                preferred_element_type=F32)
            if diagonal:
                s = jnp.where(col <= row, s, -jnp.inf)
            s_scr[h, j] = s
            mx = mx_scr[h]
            for g in range(groups):
                mx = jnp.maximum(mx, s[:, g * LANES:(g + 1) * LANES])
            mx_scr[h] = mx

    def chunkwise(n, fn):
        def four(t, carry):
            for u in range(4):
                fn(4 * t + u)
            return carry

        lax.fori_loop(0, n // 4, four, 0)
        base = (n // 4) * 4

        @pl.when(n % 4 >= 2)
        def _():
            fn(base)
            fn(base + 1)

        @pl.when(n % 2 == 1)
        def _():
            fn(n - 1)

    chunkwise(qi, lambda j: scores(j, False))
    scores(qi, True)

    for h in range(MLA_HEADS):
        mx_scr[h] = jnp.broadcast_to(jnp.max(mx_scr[h], axis=1, keepdims=True), (TQ, LANES))
    acc_scr[...] = jnp.zeros(acc_scr.shape, F32)

    def accumulate(j):
        k0 = pl.multiple_of(j * TK, TK)
        for h in range(MLA_HEADS):
            p = jnp.exp2(s_scr[h, j] - jnp.concatenate([mx_scr[h]] * groups, axis=1))
            acc_scr[h] += _dot(p.astype(BF16), v_ref[h, pl.ds(k0, TK), :])

    chunkwise(qi + 1, accumulate)

    outs = []
    for h in range(MLA_HEADS):
        acc = acc_scr[h]
        outs.append(acc[:, :V_HEAD_DIM] / acc[:, V_HEAD_DIM:V_HEAD_DIM + 1])
    o_ref[...] = jnp.concatenate(outs, axis=1).astype(BF16)


def _attention(q, k, v):
    assert TQ == TK
    stat = pltpu.VMEM((MLA_HEADS, TQ, LANES), F32)
    return pl.pallas_call(
        _attn_kernel,
        out_shape=jax.ShapeDtypeStruct((BATCH, SEQ, MLA_HEADS * V_HEAD_DIM), BF16),
        grid=(BATCH, SEQ // TQ),
        in_specs=[pl.BlockSpec((None, MLA_HEADS, TQ, HEAD_PAD), lambda b, i: (b, 0, i, 0)),
                  pl.BlockSpec((None, MLA_HEADS, SEQ, HEAD_PAD), lambda b, i: (b, 0, 0, 0)),
                  pl.BlockSpec((None, MLA_HEADS, SEQ, HEAD_PAD), lambda b, i: (b, 0, 0, 0))],
        out_specs=pl.BlockSpec((None, TQ, MLA_HEADS * V_HEAD_DIM), lambda b, i: (b, i, 0)),
        scratch_shapes=[pltpu.VMEM((MLA_HEADS, SEQ // TK, TQ, TK), F32), stat, stat],
        compiler_params=_params("parallel", "parallel"),
        name="attention",
    )(q, k, v)


def _outproj_kernel(x_ref, ya_ref, ob_ref, c_ref, halo_ref, ng_ref, wg_ref, bg_ref, dw_ref, db_ref, lng_ref, lnb_ref,
                    wa_ref, wb_ref, wc_ref, bc_ref, wo_ref, o_ref):
    x = x_ref[...]
    xn = _rms(x, ng_ref[...]).astype(BF16)

    def gate(b):
        lo = b * D_MODEL
        return jax.nn.sigmoid(_dot(xn, wg_ref[:, lo:lo + D_MODEL]) + bg_ref[:, lo:lo + D_MODEL])

    merged = gate(0) * _dot(ya_ref[...], wa_ref[...])
    merged = merged + gate(1) * _dot(ob_ref[...], wb_ref[...])
    halo = jnp.where(pl.program_id(1) == 0, 0.0, halo_ref[...])
    win = jnp.concatenate([halo, c_ref[...]], axis=0)
    acc = db_ref[...] + _causal_taps(win, dw_ref, CONV_WIDTH, CONV_PAD, TM_OUT)
    mu = jnp.mean(acc, axis=-1, keepdims=True)
    dlt = acc - mu
    var = jnp.mean(dlt * dlt, axis=-1, keepdims=True)
    yn = dlt * lax.rsqrt(var + NORM_EPS) * lng_ref[...] + lnb_ref[...]
    y_c = _dot(jax.nn.silu(yn).astype(BF16), wc_ref[...]) + bc_ref[...]
    merged = merged + gate(2) * y_c
    o_ref[...] = x + _dot(merged.astype(BF16), wo_ref[...])


def _outproj(x, ya, ob, c, w, layer):
    tile = lambda n: pl.BlockSpec((None, TM_OUT, n), lambda b, i: (b, i, 0))
    halo_blocks = TM_OUT // CONV_PAD
    halo = pl.BlockSpec((None, CONV_PAD, CONV_CH), lambda b, i: (b, jnp.maximum(i * halo_blocks - 1, 0), 0))
    consts = [w["mix_norm"], w["wg"], w["bg"], w["conv_dw_w"], w["conv_dw_b"], w["conv_ln_g"], w["conv_ln_b"],
              w["lru_w_out"], w["mla_w_o"], w["conv_w_out"], w["conv_b_out"], w["w_out"]]
    return pl.pallas_call(
        _outproj_kernel,
        out_shape=jax.ShapeDtypeStruct((BATCH, SEQ, D_MODEL), F32),
        grid=(BATCH, SEQ // TM_OUT),
        in_specs=[tile(D_MODEL), tile(LRU_WIDTH), tile(MLA_HEADS * V_HEAD_DIM), tile(CONV_CH), halo]
        + [_layer_spec(c_, layer) for c_ in consts],
        out_specs=tile(D_MODEL),
        compiler_params=_params("parallel", "parallel"),
        name="outproj",
    )(x, ya, ob, c, c, *consts)


def _prep(p):
    row = lambda v: v.reshape(DEPTH, 1, -1).astype(F32)
    o1, o2, o3 = IN_A, IN_A + IN_B, IN_A + IN_B + IN_C
    w_in, b_in = p["w_in"], p["b_in"]
    oq, okv, ope = o1, o1 + Q_LORA_RANK, o1 + Q_LORA_RANK + KV_LORA_RANK
    half = QK_ROPE_DIM // 2

    def rope_cols(m):
        z = lambda n: jnp.zeros(m.shape[:-1] + (n,), m.dtype)
        plain = jnp.concatenate([z(QK_NOPE_DIM), m, z(HEAD_PAD - QK_DIM)], axis=-1)
        rot = jnp.concatenate([z(QK_NOPE_DIM), -m[..., half:], m[..., :half], z(HEAD_PAD - QK_DIM)], axis=-1)
        return plain, rot

    wpe_plain, wpe_rot = rope_cols(w_in[..., ope:o2])
    bpe_plain, bpe_rot = rope_cols(b_in[..., ope:o2])

    w_uq = p["w_uq"].reshape(DEPTH, Q_LORA_RANK, MLA_HEADS, QK_DIM)
    zq = jnp.zeros((DEPTH, Q_LORA_RANK, MLA_HEADS, HEAD_PAD - QK_DIM), F32)
    q_plain = jnp.concatenate([w_uq, zq], axis=-1)
    q_pe = w_uq[..., QK_NOPE_DIM:]
    q_rot = jnp.concatenate([jnp.zeros((DEPTH, Q_LORA_RANK, MLA_HEADS, QK_NOPE_DIM), F32), -q_pe[..., half:],
                             q_pe[..., :half], zq], axis=-1)
    wuq = jnp.concatenate([q_plain.reshape(DEPTH, Q_LORA_RANK, -1), q_rot.reshape(DEPTH, Q_LORA_RANK, -1)], axis=-1)

    w_ukv = p["w_ukv"].reshape(DEPTH, KV_LORA_RANK, MLA_HEADS, QK_NOPE_DIM + V_HEAD_DIM)
    wk = jnp.concatenate([w_ukv[..., :QK_NOPE_DIM],
                          jnp.zeros((DEPTH, KV_LORA_RANK, MLA_HEADS, HEAD_PAD - QK_NOPE_DIM), F32)], axis=-1)
    wv = jnp.concatenate([w_ukv[..., QK_NOPE_DIM:],
                          jnp.zeros((DEPTH, KV_LORA_RANK, MLA_HEADS, HEAD_PAD - V_HEAD_DIM), F32)], axis=-1)

    wg4 = p["lru_w_gate"]
    eye = jnp.eye(LRU_HEADS, dtype=F32)
    bd = lambda blk: jnp.einsum("lhde,hg->lhdge", blk, eye).reshape(DEPTH, LRU_WIDTH, LRU_WIDTH)
    wbd = jnp.concatenate([bd(wg4[..., :LRU_HEAD_DIM]), bd(wg4[..., LRU_HEAD_DIM:])], axis=-1)
    bgate = jnp.concatenate([p["lru_b_gate"][..., :LRU_HEAD_DIM].reshape(DEPTH, -1),
                             p["lru_b_gate"][..., LRU_HEAD_DIM:].reshape(DEPTH, -1)], axis=-1)

    inv_freq = ROPE_THETA ** (-jnp.arange(0, QK_ROPE_DIM, 2, dtype=F32) / QK_ROPE_DIM)
    invf = jnp.concatenate([jnp.zeros((QK_NOPE_DIM,), F32), inv_freq, inv_freq,
                            jnp.zeros((HEAD_PAD - QK_DIM,), F32)])

    return dict(
        ffn1_norm=row(p["ffn1_norm"]), ffn2_norm=row(p["ffn2_norm"]), mix_norm=row(p["mix_norm"]),
        wa=w_in[..., :o1].astype(BF16), ba=row(b_in[..., :o1]),
        wlat=jnp.concatenate([w_in[..., oq:ope], wpe_plain, wpe_rot], axis=-1).astype(BF16),
        blat=row(jnp.concatenate([b_in[..., oq:ope], bpe_plain, bpe_rot], axis=-1)),
        wc=w_in[..., o2:o3].astype(BF16), bc=row(b_in[..., o2:o3]),
        wg=w_in[..., o3:].astype(BF16), bg=row(b_in[..., o3:]),
        q_norm=row(p["q_norm"]), wuq=(wuq * (QK_DIM ** -0.5 * LOG2_E)).astype(BF16), kv_norm=row(p["kv_norm"]),
        wukv=jnp.concatenate([wk.reshape(DEPTH, KV_LORA_RANK, -1), wv.reshape(DEPTH, KV_LORA_RANK, -1)],
                             axis=-1).astype(BF16),
        invf=invf.reshape(1, HEAD_PAD),
        lru_conv_w=p["lru_conv_w"].astype(F32), lru_conv_b=row(p["lru_conv_b"]),
        wbd=wbd.astype(BF16), bgate=row(bgate), lam=row(p["lru_lambda"]),
        conv_dw_w=p["conv_dw_w"].astype(F32), conv_dw_b=row(p["conv_dw_b"]),
        conv_ln_g=row(p["conv_ln_g"]), conv_ln_b=row(p["conv_ln_b"]),
        lru_w_out=p["lru_w_out"].astype(BF16), mla_w_o=p["mla_w_o"].astype(BF16),
        conv_w_out=p["conv_w_out"].astype(BF16), conv_b_out=row(p["conv_b_out"]), w_out=p["w_out"].astype(BF16),
    )


def kernel(x, positions, ffn1_norm, ffn1_w1, ffn1_w2, mix_norm, w_in, b_in, lru_conv_w, lru_conv_b, lru_w_gate, lru_b_gate, lru_lambda, lru_w_out, q_norm, w_uq, kv_norm, w_ukv, mla_w_o, conv_dw_w, conv_dw_b, conv_ln_g, conv_ln_b, conv_w_out, conv_b_out, w_out, ffn2_norm, ffn2_w1, ffn2_w2, final_norm):
    stacked = dict(ffn1_norm=ffn1_norm, mix_norm=mix_norm, w_in=w_in, b_in=b_in,
                   lru_conv_w=lru_conv_w, lru_conv_b=lru_conv_b, lru_w_gate=lru_w_gate, lru_b_gate=lru_b_gate,
                   lru_lambda=lru_lambda, lru_w_out=lru_w_out, q_norm=q_norm, w_uq=w_uq, kv_norm=kv_norm,
                   w_ukv=w_ukv, mla_w_o=mla_w_o, conv_dw_w=conv_dw_w, conv_dw_b=conv_dw_b, conv_ln_g=conv_ln_g,
                   conv_ln_b=conv_ln_b, conv_w_out=conv_w_out, conv_b_out=conv_b_out, w_out=w_out,
                   ffn2_norm=ffn2_norm)
    tok = BATCH * SEQ
    pos = positions.reshape(BATCH, SEQ, 1)
    final_g = final_norm.reshape(1, 1, D_MODEL).astype(F32)
    xf = x.reshape(tok, D_MODEL)
    w = _prep(stacked)
    cos, sin = _rope_tables(pos, w["invf"])
    for l in range(DEPTH):
        xf = _ffn(xf, w["ffn1_norm"], ffn1_w1, ffn1_w2, l)
        xb = xf.reshape(BATCH, SEQ, D_MODEL)
        ya, q, k, v, c = _inproj(xb, cos, sin, w, l)
        ob = _attention(q, k, v)
        xf = _outproj(xb, ya, ob, c, w, l).reshape(tok, D_MODEL)
        xf = _ffn(xf, w["ffn2_norm"], ffn2_w1, ffn2_w2, l, final_g if l == DEPTH - 1 else None)
    return xf.reshape(BATCH, SEQ, D_MODEL)
```

```python
import functools

import jax
import jax.numpy as jnp
from jax import lax
from jax.experimental import pallas as pl
from jax.experimental.pallas import tpu as pltpu

D_MODEL = 1024
BATCH = 8
SEQ = 2048
DEPTH = 2
D_FF = 2816
NORM_EPS = 1e-6
LRU_WIDTH = 512
LRU_HEADS = 8
LRU_HEAD_DIM = LRU_WIDTH // LRU_HEADS
LRU_CONV_WIDTH = 4
LRU_C = 8.0
MLA_HEADS = 8
QK_NOPE_DIM = 64
QK_ROPE_DIM = 32
V_HEAD_DIM = 64
Q_LORA_RANK = 384
KV_LORA_RANK = 256
ROPE_THETA = 10000.0
CONV_CH = 512
CONV_WIDTH = 31
N_BRANCH = 3
IN_A = 2 * LRU_WIDTH
IN_B = Q_LORA_RANK + KV_LORA_RANK + QK_ROPE_DIM
IN_C = 2 * CONV_CH
IN_G = N_BRANCH * D_MODEL

LANES = 128
SUBLANES = 8
HEAD_PAD = LANES
QK_DIM = QK_NOPE_DIM + QK_ROPE_DIM
LOG2_E = 1.4426950408889634
VMEM_LIMIT = 56 * 1024 * 1024

TM_FFN = 512
FF_CHUNK = 256
TM_IN = 512
TM_OUT = 512
CONV_PAD = 32
LRU_PAD = SUBLANES
TQ = 256
TK = 256

F32 = jnp.float32
BF16 = jnp.bfloat16


def _layer_spec(stacked, layer):
    nd = stacked.ndim - 1
    return pl.BlockSpec((None,) + stacked.shape[1:], lambda *_: (layer,) + (0,) * nd, pipeline_mode=pl.Buffered(1))


def _params(*semantics):
    return pltpu.CompilerParams(dimension_semantics=semantics, vmem_limit_bytes=VMEM_LIMIT)


def _rms(x, g):
    return x * lax.rsqrt(jnp.mean(x * x, axis=-1, keepdims=True) + NORM_EPS) * g


def _dot(a, b):
    return jnp.dot(a, b, preferred_element_type=F32)


def _ffn_slab_copies(w1_hbm, w2_hbm, layer, c, stages, sem):
    slot, lo = c % 2, c * FF_CHUNK
    srcs = (w1_hbm.at[layer, :, pl.ds(lo, FF_CHUNK)], w1_hbm.at[layer, :, pl.ds(D_FF + lo, FF_CHUNK)],
            w2_hbm.at[layer, pl.ds(lo, FF_CHUNK), :])
    return [pltpu.make_async_copy(src, stage.at[slot], sem.at[n, slot])
            for n, (src, stage) in enumerate(zip(srcs, stages))]


def _ffn_kernel(x_ref, g_ref, w1_hbm, w2_hbm, *rest, layer, final):
    if final:
        fg_ref, o_ref, w1_ref, w2_ref, stage_g, stage_u, stage_d, sem = rest
    else:
        o_ref, w1_ref, w2_ref, stage_g, stage_u, stage_d, sem = rest
    stages = (stage_g, stage_u, stage_d)
    n_chunks = D_FF // FF_CHUNK
    copies = lambda c: _ffn_slab_copies(w1_hbm, w2_hbm, layer, c, stages, sem)

    def tile(load_weights):
        x = x_ref[...]
        xn = _rms(x, g_ref[...]).astype(BF16)
        if load_weights:
            for cp in copies(0):
                cp.start()
        acc = None
        for c in range(n_chunks):
            lo = c * FF_CHUNK
            if load_weights:
                if c + 1 < n_chunks:
                    for cp in copies(c + 1):
                        cp.start()
                for cp in copies(c):
                    cp.wait()
                w1_ref[:, lo:lo + FF_CHUNK] = stage_g[c % 2].astype(BF16)
                w1_ref[:, D_FF + lo:D_FF + lo + FF_CHUNK] = stage_u[c % 2].astype(BF16)
                w2_ref[lo:lo + FF_CHUNK, :] = stage_d[c % 2].astype(BF16)
            g = _dot(xn, w1_ref[:, lo:lo + FF_CHUNK])
            u = _dot(xn, w1_ref[:, D_FF + lo:D_FF + lo + FF_CHUNK])
            h = (jax.nn.silu(g) * u).astype(BF16)
            d = _dot(h, w2_ref[lo:lo + FF_CHUNK, :])
            acc = d if acc is None else acc + d
        y = x + 0.5 * acc
        if final:
            y = _rms(y, fg_ref[...])
        o_ref[...] = y

    first_step = pl.program_id(0) == 0

    @pl.when(first_step)
    def _():
        tile(True)

    @pl.when(jnp.logical_not(first_step))
    def _():
        tile(False)


def _ffn(x, norm_g, w1_stack, w2_stack, layer, final_g=None):
    tok = x.shape[0]
    final = final_g is not None
    hbm = pl.BlockSpec(memory_space=pl.ANY)
    in_specs = [pl.BlockSpec((TM_FFN, D_MODEL), lambda i: (i, 0)), _layer_spec(norm_g, layer), hbm, hbm]
    args = [x, norm_g, w1_stack, w2_stack]
    if final:
        in_specs.append(_layer_spec(final_g, 0))
        args.append(final_g)
    return pl.pallas_call(
        functools.partial(_ffn_kernel, layer=layer, final=final),
        out_shape=jax.ShapeDtypeStruct((tok, D_MODEL), F32),
        grid=(tok // TM_FFN,),
        in_specs=in_specs,
        out_specs=pl.BlockSpec((TM_FFN, D_MODEL), lambda i: (i, 0)),
        scratch_shapes=[pltpu.VMEM((D_MODEL, 2 * D_FF), BF16), pltpu.VMEM((D_FF, D_MODEL), BF16),
                        pltpu.VMEM((2, D_MODEL, FF_CHUNK), F32), pltpu.VMEM((2, D_MODEL, FF_CHUNK), F32),
                        pltpu.VMEM((2, FF_CHUNK, D_MODEL), F32), pltpu.SemaphoreType.DMA((3, 2))],
        compiler_params=_params("arbitrary"),
        name="ffn_final" if final else "ffn",
    )(*args)


def _causal_taps(win, w_ref, n_taps, pad, rows):
    n = win.shape[0]
    offs = [pad - (n_taps - 1) + j for j in range(n_taps)]
    acc = None
    for r in range(SUBLANES):
        taps = [j for j in range(n_taps) if offs[j] % SUBLANES == r]
        if not taps:
            continue
        shifted = win if r == 0 else pltpu.roll(win, n - r, axis=0)
        for j in taps:
            lo = offs[j] - r
            term = w_ref[j:j + 1, :] * shifted[lo:lo + rows, :]
            acc = term if acc is None else acc + term
    return acc


def _linear_scan(a, u, h0):
    rows, width = a.shape
    row = lax.broadcasted_iota(jnp.int32, (SUBLANES, width), 0)
    keeps = [(d, row >= d) for d in (1, 2, 4)]
    h, out = h0, []
    for k in range(rows // SUBLANES):
        av = a[k * SUBLANES:(k + 1) * SUBLANES, :]
        hv = u[k * SUBLANES:(k + 1) * SUBLANES, :]
        for d, keep in keeps:
            hv = jnp.where(keep, av * pltpu.roll(hv, d, axis=0) + hv, hv)
            av = jnp.where(keep, av * pltpu.roll(av, d, axis=0), av)
        hv = hv + av * h
        out.append(hv)
        h = jnp.broadcast_to(hv[SUBLANES - 1:SUBLANES, :], (SUBLANES, width))
    return jnp.concatenate(out, axis=0), h


def _inproj_kernel(x_ref, cos_ref, sin_ref, ng_ref, wa_ref, ba_ref, wlat_ref, blat_ref, wc_ref, bc_ref,
                   qn_ref, wuq_ref, kvn_ref, wukv_ref, cw_ref, cb_ref, wbd_ref, bgate_ref, lam_ref,
                   ya_ref, q_ref, k_ref, v_ref, c_ref, tail_scr, h_scr):
    @pl.when(pl.program_id(1) == 0)
    def _():
        tail_scr[...] = jnp.zeros(tail_scr.shape, F32)
        h_scr[...] = jnp.zeros(h_scr.shape, F32)

    xn = _rms(x_ref[...], ng_ref[...]).astype(BF16)
    pa = _dot(xn, wa_ref[...]) + ba_ref[...]
    xa_pre = pa[:, :LRU_WIDTH]
    win = jnp.concatenate([tail_scr[...], xa_pre], axis=0)
    tail_scr[...] = xa_pre[TM_IN - LRU_PAD:, :]
    xa = cb_ref[...] + _causal_taps(win, cw_ref, LRU_CONV_WIDTH, LRU_PAD, TM_IN)
    gates = _dot(xa.astype(BF16), wbd_ref[...]) + bgate_ref[...]
    r = jax.nn.sigmoid(gates[:, :LRU_WIDTH])
    ig = jax.nn.sigmoid(gates[:, LRU_WIDTH:])
    log_a = (-LRU_C) * r * jax.nn.softplus(-lam_ref[...])
    a = jnp.exp(log_a)
    one_m = (1.0 + a * a) * jnp.tanh(-log_a)
    root = jnp.where(one_m > 0.0, one_m * lax.rsqrt(one_m), 0.0)
    h, h_scr[...] = _linear_scan(a, root * (ig * xa), h_scr[...])
    ya_ref[...] = (h * jax.nn.gelu(pa[:, LRU_WIDTH:])).astype(BF16)
    pc = _dot(xn, wc_ref[...]) + bc_ref[...]
    c_ref[...] = pc[:, :CONV_CH] * jax.nn.sigmoid(pc[:, CONV_CH:])
    lat = _dot(xn, wlat_ref[...]) + blat_ref[...]
    okv, ope = Q_LORA_RANK, Q_LORA_RANK + KV_LORA_RANK
    qq = _dot(_rms(lat[:, :okv], qn_ref[...]).astype(BF16), wuq_ref[...])
    kv = _dot(_rms(lat[:, okv:ope], kvn_ref[...]).astype(BF16), wukv_ref[...])
    cosf, sinf = cos_ref[...], sin_ref[...]
    kpe = lat[:, ope:ope + HEAD_PAD] * cosf + lat[:, ope + HEAD_PAD:] * sinf
    nq = MLA_HEADS * HEAD_PAD
    ones_lane = lax.broadcasted_iota(jnp.int32, (TM_IN, HEAD_PAD), 1) == V_HEAD_DIM
    for hd in range(MLA_HEADS):
        lo = hd * HEAD_PAD
        q_ref[hd] = (qq[:, lo:lo + HEAD_PAD] * cosf + qq[:, nq + lo:nq + lo + HEAD_PAD] * sinf).astype(BF16)
        k_ref[hd] = (kv[:, lo:lo + HEAD_PAD] + kpe).astype(BF16)
        v_ref[hd] = jnp.where(ones_lane, 1.0, kv[:, nq + lo:nq + lo + HEAD_PAD]).astype(BF16)


def _rope_kernel(pos_ref, invf_ref, cos_ref, sin_ref):
    ang = pos_ref[...].astype(F32) * invf_ref[...]
    cos_ref[...] = jnp.cos(ang)
    sin_ref[...] = jnp.sin(ang)


def _rope_tables(pos, invf):
    tile = lambda n: pl.BlockSpec((None, TM_IN, n), lambda b, i: (b, i, 0))
    table = jax.ShapeDtypeStruct((BATCH, SEQ, HEAD_PAD), F32)
    return pl.pallas_call(
        _rope_kernel,
        out_shape=(table, table),
        grid=(BATCH, SEQ // TM_IN),
        in_specs=[tile(1), pl.BlockSpec((1, HEAD_PAD), lambda b, i: (0, 0))],
        out_specs=(tile(HEAD_PAD), tile(HEAD_PAD)),
        compiler_params=_params("parallel", "parallel"),
        name="rope_tables",
    )(pos, invf)


def _inproj(x, cos, sin, w, layer):
    nt = SEQ // TM_IN
    tile = lambda n: pl.BlockSpec((None, TM_IN, n), lambda b, i: (b, i, 0))
    heads = lambda n: pl.BlockSpec((None, n, TM_IN, LANES), lambda b, i: (b, 0, i, 0))
    consts = [w["mix_norm"], w["wa"], w["ba"], w["wlat"], w["blat"], w["wc"], w["bc"],
              w["q_norm"], w["wuq"], w["kv_norm"], w["wukv"],
              w["lru_conv_w"], w["lru_conv_b"], w["wbd"], w["bgate"], w["lam"]]
    carry = pltpu.VMEM((LRU_PAD, LRU_WIDTH), F32)
    return pl.pallas_call(
        _inproj_kernel,
        out_shape=(jax.ShapeDtypeStruct((BATCH, SEQ, LRU_WIDTH), BF16),
                   jax.ShapeDtypeStruct((BATCH, MLA_HEADS, SEQ, HEAD_PAD), BF16),
                   jax.ShapeDtypeStruct((BATCH, MLA_HEADS, SEQ, HEAD_PAD), BF16),
                   jax.ShapeDtypeStruct((BATCH, MLA_HEADS, SEQ, HEAD_PAD), BF16),
                   jax.ShapeDtypeStruct((BATCH, SEQ, CONV_CH), F32)),
        grid=(BATCH, nt),
        in_specs=[tile(D_MODEL), tile(HEAD_PAD), tile(HEAD_PAD)] + [_layer_spec(c, layer) for c in consts],
        out_specs=(tile(LRU_WIDTH), heads(MLA_HEADS), heads(MLA_HEADS), heads(MLA_HEADS), tile(CONV_CH)),
        scratch_shapes=[carry, carry],
        compiler_params=_params("parallel", "arbitrary"),
        name="inproj",
    )(x, cos, sin, *consts)


def _attn_kernel(q_ref, k_ref, v_ref, o_ref, s_scr, mx_scr, acc_scr):
    qi = pl.program_id(1)
    groups = TK // LANES
    row = lax.broadcasted_iota(jnp.int32, (TQ, TK), 0)
    col = lax.broadcasted_iota(jnp.int32, (TQ, TK), 1)
    mx_scr[...] = jnp.full(mx_scr.shape, -jnp.inf, F32)

    def scores(j, diagonal):
        k0 = pl.multiple_of(j * TK, TK)
        for h in range(MLA_HEADS):
            s = lax.dot_general(q_ref[h], k_ref[h, pl.ds(k0, TK), :], (((1,), (1,)), ((), ())),
                                preferred_element_type=F32)
            if diagonal:
                s = jnp.where(col <= row, s, -jnp.inf)
            s_scr[h, j] = s
            mx = mx_scr[h]
            for g in range(groups):
                mx = jnp.maximum(mx, s[:, g * LANES:(g + 1) * LANES])
            mx_scr[h] = mx

    def chunkwise(n, fn):
        def four(t, carry):
            for u in range(4):
                fn(4 * t + u)
            return carry

        lax.fori_loop(0, n // 4, four, 0)
        base = (n // 4) * 4

        @pl.when(n % 4 >= 2)
        def _():
            fn(base)
            fn(base + 1)

        @pl.when(n % 2 == 1)
        def _():
            fn(n - 1)

    chunkwise(qi, lambda j: scores(j, False))
    scores(qi, True)

    for h in range(MLA_HEADS):
        mx_scr[h] = jnp.broadcast_to(jnp.max(mx_scr[h], axis=1, keepdims=True), (TQ, LANES))
    acc_scr[...] = jnp.zeros(acc_scr.shape, F32)

    def accumulate(j):
        k0 = pl.multiple_of(j * TK, TK)
        for h in range(MLA_HEADS):
            p = jnp.exp2(s_scr[h, j] - jnp.concatenate([mx_scr[h]] * groups, axis=1))
            acc_scr[h] += _dot(p.astype(BF16), v_ref[h, pl.ds(k0, TK), :])

    chunkwise(qi + 1, accumulate)

    outs = []
    for h in range(MLA_HEADS):
        acc = acc_scr[h]
        outs.append(acc[:, :V_HEAD_DIM] / acc[:, V_HEAD_DIM:V_HEAD_DIM + 1])
    o_ref[...] = jnp.concatenate(outs, axis=1).astype(BF16)


def _attention(q, k, v):
    assert TQ == TK
    stat = pltpu.VMEM((MLA_HEADS, TQ, LANES), F32)
    return pl.pallas_call(
        _attn_kernel,
        out_shape=jax.ShapeDtypeStruct((BATCH, SEQ, MLA_HEADS * V_HEAD_DIM), BF16),
        grid=(BATCH, SEQ // TQ),
        in_specs=[pl.BlockSpec((None, MLA_HEADS, TQ, HEAD_PAD), lambda b, i: (b, 0, i, 0)),
                  pl.BlockSpec((None, MLA_HEADS, SEQ, HEAD_PAD), lambda b, i: (b, 0, 0, 0)),
                  pl.BlockSpec((None, MLA_HEADS, SEQ, HEAD_PAD), lambda b, i: (b, 0, 0, 0))],
        out_specs=pl.BlockSpec((None, TQ, MLA_HEADS * V_HEAD_DIM), lambda b, i: (b, i, 0)),
        scratch_shapes=[pltpu.VMEM((MLA_HEADS, SEQ // TK, TQ, TK), F32), stat, stat],
        compiler_params=_params("parallel", "parallel"),
        name="attention",
    )(q, k, v)


def _outproj_kernel(x_ref, ya_ref, ob_ref, c_ref, halo_ref, ng_ref, wg_ref, bg_ref, dw_ref, db_ref, lng_ref, lnb_ref,
                    wa_ref, wb_ref, wc_ref, bc_ref, wo_ref, o_ref):
    x = x_ref[...]
    xn = _rms(x, ng_ref[...]).astype(BF16)

    def gate(b):
        lo = b * D_MODEL
        return jax.nn.sigmoid(_dot(xn, wg_ref[:, lo:lo + D_MODEL]) + bg_ref[:, lo:lo + D_MODEL])

    merged = gate(0) * _dot(ya_ref[...], wa_ref[...])
    merged = merged + gate(1) * _dot(ob_ref[...], wb_ref[...])
    halo = jnp.where(pl.program_id(1) == 0, 0.0, halo_ref[...])
    win = jnp.concatenate([halo, c_ref[...]], axis=0)
    acc = db_ref[...] + _causal_taps(win, dw_ref, CONV_WIDTH, CONV_PAD, TM_OUT)
    mu = jnp.mean(acc, axis=-1, keepdims=True)
    dlt = acc - mu
    var = jnp.mean(dlt * dlt, axis=-1, keepdims=True)
    yn = dlt * lax.rsqrt(var + NORM_EPS) * lng_ref[...] + lnb_ref[...]
    y_c = _dot(jax.nn.silu(yn).astype(BF16), wc_ref[...]) + bc_ref[...]
    merged = merged + gate(2) * y_c
    o_ref[...] = x + _dot(merged.astype(BF16), wo_ref[...])


def _outproj(x, ya, ob, c, w, layer):
    tile = lambda n: pl.BlockSpec((None, TM_OUT, n), lambda b, i: (b, i, 0))
    halo_blocks = TM_OUT // CONV_PAD
    halo = pl.BlockSpec((None, CONV_PAD, CONV_CH), lambda b, i: (b, jnp.maximum(i * halo_blocks - 1, 0), 0))
    consts = [w["mix_norm"], w["wg"], w["bg"], w["conv_dw_w"], w["conv_dw_b"], w["conv_ln_g"], w["conv_ln_b"],
              w["lru_w_out"], w["mla_w_o"], w["conv_w_out"], w["conv_b_out"], w["w_out"]]
    return pl.pallas_call(
        _outproj_kernel,
        out_shape=jax.ShapeDtypeStruct((BATCH, SEQ, D_MODEL), F32),
        grid=(BATCH, SEQ // TM_OUT),
        in_specs=[tile(D_MODEL), tile(LRU_WIDTH), tile(MLA_HEADS * V_HEAD_DIM), tile(CONV_CH), halo]
        + [_layer_spec(c_, layer) for c_ in consts],
        out_specs=tile(D_MODEL),
        compiler_params=_params("parallel", "parallel"),
        name="outproj",
    )(x, ya, ob, c, c, *consts)


def _prep(p):
    row = lambda v: v.reshape(DEPTH, 1, -1).astype(F32)
    o1, o2, o3 = IN_A, IN_A + IN_B, IN_A + IN_B + IN_C
    w_in, b_in = p["w_in"], p["b_in"]
    oq, okv, ope = o1, o1 + Q_LORA_RANK, o1 + Q_LORA_RANK + KV_LORA_RANK
    half = QK_ROPE_DIM // 2

    def rope_cols(m):
        z = lambda n: jnp.zeros(m.shape[:-1] + (n,), m.dtype)
        plain = jnp.concatenate([z(QK_NOPE_DIM), m, z(HEAD_PAD - QK_DIM)], axis=-1)
        rot = jnp.concatenate([z(QK_NOPE_DIM), -m[..., half:], m[..., :half], z(HEAD_PAD - QK_DIM)], axis=-1)
        return plain, rot

    wpe_plain, wpe_rot = rope_cols(w_in[..., ope:o2])
    bpe_plain, bpe_rot = rope_cols(b_in[..., ope:o2])

    w_uq = p["w_uq"].reshape(DEPTH, Q_LORA_RANK, MLA_HEADS, QK_DIM)
    zq = jnp.zeros((DEPTH, Q_LORA_RANK, MLA_HEADS, HEAD_PAD - QK_DIM), F32)
    q_plain = jnp.concatenate([w_uq, zq], axis=-1)
    q_pe = w_uq[..., QK_NOPE_DIM:]
    q_rot = jnp.concatenate([jnp.zeros((DEPTH, Q_LORA_RANK, MLA_HEADS, QK_NOPE_DIM), F32), -q_pe[..., half:],
                             q_pe[..., :half], zq], axis=-1)
    wuq = jnp.concatenate([q_plain.reshape(DEPTH, Q_LORA_RANK, -1), q_rot.reshape(DEPTH, Q_LORA_RANK, -1)], axis=-1)

    w_ukv = p["w_ukv"].reshape(DEPTH, KV_LORA_RANK, MLA_HEADS, QK_NOPE_DIM + V_HEAD_DIM)
    wk = jnp.concatenate([w_ukv[..., :QK_NOPE_DIM],
                          jnp.zeros((DEPTH, KV_LORA_RANK, MLA_HEADS, HEAD_PAD - QK_NOPE_DIM), F32)], axis=-1)
    wv = jnp.concatenate([w_ukv[..., QK_NOPE_DIM:],
                          jnp.zeros((DEPTH, KV_LORA_RANK, MLA_HEADS, HEAD_PAD - V_HEAD_DIM), F32)], axis=-1)

    wg4 = p["lru_w_gate"]
    eye = jnp.eye(LRU_HEADS, dtype=F32)
    bd = lambda blk: jnp.einsum("lhde,hg->lhdge", blk, eye).reshape(DEPTH, LRU_WIDTH, LRU_WIDTH)
    wbd = jnp.concatenate([bd(wg4[..., :LRU_HEAD_DIM]), bd(wg4[..., LRU_HEAD_DIM:])], axis=-1)
    bgate = jnp.concatenate([p["lru_b_gate"][..., :LRU_HEAD_DIM].reshape(DEPTH, -1),
                             p["lru_b_gate"][..., LRU_HEAD_DIM:].reshape(DEPTH, -1)], axis=-1)

    inv_freq = ROPE_THETA ** (-jnp.arange(0, QK_ROPE_DIM, 2, dtype=F32) / QK_ROPE_DIM)
    invf = jnp.concatenate([jnp.zeros((QK_NOPE_DIM,), F32), inv_freq, inv_freq,
                            jnp.zeros((HEAD_PAD - QK_DIM,), F32)])

    return dict(
        ffn1_norm=row(p["ffn1_norm"]), ffn2_norm=row(p["ffn2_norm"]), mix_norm=row(p["mix_norm"]),
        wa=w_in[..., :o1].astype(BF16), ba=row(b_in[..., :o1]),
        wlat=jnp.concatenate([w_in[..., oq:ope], wpe_plain, wpe_rot], axis=-1).astype(BF16),
        blat=row(jnp.concatenate([b_in[..., oq:ope], bpe_plain, bpe_rot], axis=-1)),
        wc=w_in[..., o2:o3].astype(BF16), bc=row(b_in[..., o2:o3]),
        wg=w_in[..., o3:].astype(BF16), bg=row(b_in[..., o3:]),
        q_norm=row(p["q_norm"]), wuq=(wuq * (QK_DIM ** -0.5 * LOG2_E)).astype(BF16), kv_norm=row(p["kv_norm"]),
        wukv=jnp.concatenate([wk.reshape(DEPTH, KV_LORA_RANK, -1), wv.reshape(DEPTH, KV_LORA_RANK, -1)],
                             axis=-1).astype(BF16),
        invf=invf.reshape(1, HEAD_PAD),
        lru_conv_w=p["lru_conv_w"].astype(F32), lru_conv_b=row(p["lru_conv_b"]),
        wbd=wbd.astype(BF16), bgate=row(bgate), lam=row(p["lru_lambda"]),
        conv_dw_w=p["conv_dw_w"].astype(F32), conv_dw_b=row(p["conv_dw_b"]),
        conv_ln_g=row(p["conv_ln_g"]), conv_ln_b=row(p["conv_ln_b"]),
        lru_w_out=p["lru_w_out"].astype(BF16), mla_w_o=p["mla_w_o"].astype(BF16),
        conv_w_out=p["conv_w_out"].astype(BF16), conv_b_out=row(p["conv_b_out"]), w_out=p["w_out"].astype(BF16),
    )


def kernel(x, positions, ffn1_norm, ffn1_w1, ffn1_w2, mix_norm, w_in, b_in, lru_conv_w, lru_conv_b, lru_w_gate, lru_b_gate, lru_lambda, lru_w_out, q_norm, w_uq, kv_norm, w_ukv, mla_w_o, conv_dw_w, conv_dw_b, conv_ln_g, conv_ln_b, conv_w_out, conv_b_out, w_out, ffn2_norm, ffn2_w1, ffn2_w2, final_norm):
    stacked = dict(ffn1_norm=ffn1_norm, mix_norm=mix_norm, w_in=w_in, b_in=b_in,
                   lru_conv_w=lru_conv_w, lru_conv_b=lru_conv_b, lru_w_gate=lru_w_gate, lru_b_gate=lru_b_gate,
                   lru_lambda=lru_lambda, lru_w_out=lru_w_out, q_norm=q_norm, w_uq=w_uq, kv_norm=kv_norm,
                   w_ukv=w_ukv, mla_w_o=mla_w_o, conv_dw_w=conv_dw_w, conv_dw_b=conv_dw_b, conv_ln_g=conv_ln_g,
                   conv_ln_b=conv_ln_b, conv_w_out=conv_w_out, conv_b_out=conv_b_out, w_out=w_out,
                   ffn2_norm=ffn2_norm)
    tok = BATCH * SEQ
    pos = positions.reshape(BATCH, SEQ, 1)
    final_g = final_norm.reshape(1, 1, D_MODEL).astype(F32)
    xf = x.reshape(tok, D_MODEL)
    w = _prep(stacked)
    cos, sin = _rope_tables(pos, w["invf"])
    for l in range(DEPTH):
        xf = _ffn(xf, w["ffn1_norm"], ffn1_w1, ffn1_w2, l)
        xb = xf.reshape(BATCH, SEQ, D_MODEL)
        ya, q, k, v, c = _inproj(xb, cos, sin, w, l)
        ob = _attention(q, k, v)
        xf = _outproj(xb, ya, ob, c, w, l).reshape(tok, D_MODEL)
        xf = _ffn(xf, w["ffn2_norm"], ffn2_w1, ffn2_w2, l, final_g if l == DEPTH - 1 else None)
    return xf.reshape(BATCH, SEQ, D_MODEL)
```

```python
import functools

import jax
import jax.numpy as jnp
from jax import lax
from jax.experimental import pallas as pl
from jax.experimental.pallas import tpu as pltpu

D_MODEL = 1024
BATCH = 8
SEQ = 2048
DEPTH = 2
D_FF = 2816
NORM_EPS = 1e-6
LRU_WIDTH = 512
LRU_HEADS = 8
LRU_HEAD_DIM = LRU_WIDTH // LRU_HEADS
LRU_CONV_WIDTH = 4
LRU_C = 8.0
MLA_HEADS = 8
QK_NOPE_DIM = 64
QK_ROPE_DIM = 32
V_HEAD_DIM = 64
Q_LORA_RANK = 384
KV_LORA_RANK = 256
ROPE_THETA = 10000.0
CONV_CH = 512
CONV_WIDTH = 31
N_BRANCH = 3
IN_A = 2 * LRU_WIDTH
IN_B = Q_LORA_RANK + KV_LORA_RANK + QK_ROPE_DIM
IN_C = 2 * CONV_CH
IN_G = N_BRANCH * D_MODEL

LANES = 128
SUBLANES = 8
HEAD_PAD = LANES
QK_DIM = QK_NOPE_DIM + QK_ROPE_DIM
LOG2_E = 1.4426950408889634
VMEM_LIMIT = 56 * 1024 * 1024

TM_FFN = 512
FF_CHUNK = 256
TM_IN = 512
TM_OUT = 512
CONV_PAD = 32
LRU_PAD = SUBLANES
TQ = 256
TK = 256

F32 = jnp.float32
BF16 = jnp.bfloat16


def _layer_spec(stacked, layer):
    nd = stacked.ndim - 1
    return pl.BlockSpec((None,) + stacked.shape[1:], lambda *_: (layer,) + (0,) * nd, pipeline_mode=pl.Buffered(1))


def _params(*semantics):
    return pltpu.CompilerParams(dimension_semantics=semantics, vmem_limit_bytes=VMEM_LIMIT)


def _rms(x, g):
    return x * lax.rsqrt(jnp.mean(x * x, axis=-1, keepdims=True) + NORM_EPS) * g


def _dot(a, b):
    return jnp.dot(a, b, preferred_element_type=F32)


def _ffn_slab_copies(w1_hbm, w2_hbm, layer, c, stages, sem):
    slot, lo = c % 2, c * FF_CHUNK
    srcs = (w1_hbm.at[layer, :, pl.ds(lo, FF_CHUNK)], w1_hbm.at[layer, :, pl.ds(D_FF + lo, FF_CHUNK)],
            w2_hbm.at[layer, pl.ds(lo, FF_CHUNK), :])
    return [pltpu.make_async_copy(src, stage.at[slot], sem.at[n, slot])
            for n, (src, stage) in enumerate(zip(srcs, stages))]


def _ffn_kernel(x_ref, g_ref, w1_hbm, w2_hbm, *rest, layer, final):
    if final:
        fg_ref, o_ref, w1_ref, w2_ref, stage_g, stage_u, stage_d, sem = rest
    else:
        o_ref, w1_ref, w2_ref, stage_g, stage_u, stage_d, sem = rest
    stages = (stage_g, stage_u, stage_d)
    n_chunks = D_FF // FF_CHUNK
    copies = lambda c: _ffn_slab_copies(w1_hbm, w2_hbm, layer, c, stages, sem)

    def tile(load_weights):
        x = x_ref[...]
        xn = _rms(x, g_ref[...]).astype(BF16)
        if load_weights:
            for cp in copies(0):
                cp.start()
        acc = None
        for c in range(n_chunks):
            lo = c * FF_CHUNK
            if load_weights:
                if c + 1 < n_chunks:
                    for cp in copies(c + 1):
                        cp.start()
                for cp in copies(c):
                    cp.wait()
                w1_ref[:, lo:lo + FF_CHUNK] = stage_g[c % 2].astype(BF16)
                w1_ref[:, D_FF + lo:D_FF + lo + FF_CHUNK] = stage_u[c % 2].astype(BF16)
                w2_ref[lo:lo + FF_CHUNK, :] = stage_d[c % 2].astype(BF16)
            g = _dot(xn, w1_ref[:, lo:lo + FF_CHUNK])
            u = _dot(xn, w1_ref[:, D_FF + lo:D_FF + lo + FF_CHUNK])
            h = (jax.nn.silu(g) * u).astype(BF16)
            d = _dot(h, w2_ref[lo:lo + FF_CHUNK, :])
            acc = d if acc is None else acc + d
        y = x + 0.5 * acc
        if final:
            y = _rms(y, fg_ref[...])
        o_ref[...] = y

    first_step = pl.program_id(0) == 0

    @pl.when(first_step)
    def _():
        tile(True)

    @pl.when(jnp.logical_not(first_step))
    def _():
        tile(False)


def _ffn(x, norm_g, w1_stack, w2_stack, layer, final_g=None):
    tok = x.shape[0]
    final = final_g is not None
    hbm = pl.BlockSpec(memory_space=pl.ANY)
    in_specs = [pl.BlockSpec((TM_FFN, D_MODEL), lambda i: (i, 0)), _layer_spec(norm_g, layer), hbm, hbm]
    args = [x, norm_g, w1_stack, w2_stack]
    if final:
        in_specs.append(_layer_spec(final_g, 0))
        args.append(final_g)
    return pl.pallas_call(
        functools.partial(_ffn_kernel, layer=layer, final=final),
        out_shape=jax.ShapeDtypeStruct((tok, D_MODEL), F32),
        grid=(tok // TM_FFN,),
        in_specs=in_specs,
        out_specs=pl.BlockSpec((TM_FFN, D_MODEL), lambda i: (i, 0)),
        scratch_shapes=[pltpu.VMEM((D_MODEL, 2 * D_FF), BF16), pltpu.VMEM((D_FF, D_MODEL), BF16),
                        pltpu.VMEM((2, D_MODEL, FF_CHUNK), F32), pltpu.VMEM((2, D_MODEL, FF_CHUNK), F32),
                        pltpu.VMEM((2, FF_CHUNK, D_MODEL), F32), pltpu.SemaphoreType.DMA((3, 2))],
        compiler_params=_params("arbitrary"),
        name="ffn_final" if final else "ffn",
    )(*args)


def _causal_taps(win, w_ref, n_taps, pad, rows):
    n = win.shape[0]
    offs = [pad - (n_taps - 1) + j for j in range(n_taps)]
    acc = None
    for r in range(SUBLANES):
        taps = [j for j in range(n_taps) if offs[j] % SUBLANES == r]
        if not taps:
            continue
        shifted = win if r == 0 else pltpu.roll(win, n - r, axis=0)
        for j in taps:
            lo = offs[j] - r
            term = w_ref[j:j + 1, :] * shifted[lo:lo + rows, :]
            acc = term if acc is None else acc + term
    return acc


def _linear_scan(a, u, h0):
    rows, width = a.shape
    row = lax.broadcasted_iota(jnp.int32, (SUBLANES, width), 0)
    keeps = [(d, row >= d) for d in (1, 2, 4)]
    h, out = h0, []
    for k in range(rows // SUBLANES):
        av = a[k * SUBLANES:(k + 1) * SUBLANES, :]
        hv = u[k * SUBLANES:(k + 1) * SUBLANES, :]
        for d, keep in keeps:
            hv = jnp.where(keep, av * pltpu.roll(hv, d, axis=0) + hv, hv)
            av = jnp.where(keep, av * pltpu.roll(av, d, axis=0), av)
        hv = hv + av * h
        out.append(hv)
        h = jnp.broadcast_to(hv[SUBLANES - 1:SUBLANES, :], (SUBLANES, width))
    return jnp.concatenate(out, axis=0), h


def _inproj_kernel(x_ref, cos_ref, sin_ref, ng_ref, wa_ref, ba_ref, wlat_ref, blat_ref, wc_ref, bc_ref,
                   qn_ref, wuq_ref, kvn_ref, wukv_ref, cw_ref, cb_ref, wbd_ref, bgate_ref, lam_ref,
                   ya_ref, q_ref, k_ref, v_ref, c_ref, tail_scr, h_scr):
    @pl.when(pl.program_id(1) == 0)
    def _():
        tail_scr[...] = jnp.zeros(tail_scr.shape, F32)
        h_scr[...] = jnp.zeros(h_scr.shape, F32)

    xn = _rms(x_ref[...], ng_ref[...]).astype(BF16)
    pa = _dot(xn, wa_ref[...]) + ba_ref[...]
    xa_pre = pa[:, :LRU_WIDTH]
    win = jnp.concatenate([tail_scr[...], xa_pre], axis=0)
    tail_scr[...] = xa_pre[TM_IN - LRU_PAD:, :]
    xa = cb_ref[...] + _causal_taps(win, cw_ref, LRU_CONV_WIDTH, LRU_PAD, TM_IN)
    gates = _dot(xa.astype(BF16), wbd_ref[...]) + bgate_ref[...]
    r = jax.nn.sigmoid(gates[:, :LRU_WIDTH])
    ig = jax.nn.sigmoid(gates[:, LRU_WIDTH:])
    log_a = (-LRU_C) * r * jax.nn.softplus(-lam_ref[...])
    a = jnp.exp(log_a)
    one_m = (1.0 + a * a) * jnp.tanh(-log_a)
    root = jnp.where(one_m > 0.0, one_m * lax.rsqrt(one_m), 0.0)
    h, h_scr[...] = _linear_scan(a, root * (ig * xa), h_scr[...])
    ya_ref[...] = (h * jax.nn.gelu(pa[:, LRU_WIDTH:])).astype(BF16)
    pc = _dot(xn, wc_ref[...]) + bc_ref[...]
    c_ref[...] = pc[:, :CONV_CH] * jax.nn.sigmoid(pc[:, CONV_CH:])
    lat = _dot(xn, wlat_ref[...]) + blat_ref[...]
    okv, ope = Q_LORA_RANK, Q_LORA_RANK + KV_LORA_RANK
    qq = _dot(_rms(lat[:, :okv], qn_ref[...]).astype(BF16), wuq_ref[...])
    kv = _dot(_rms(lat[:, okv:ope], kvn_ref[...]).astype(BF16), wukv_ref[...])
    cosf, sinf = cos_ref[...], sin_ref[...]
    kpe = lat[:, ope:ope + HEAD_PAD] * cosf + lat[:, ope + HEAD_PAD:] * sinf
    nq = MLA_HEADS * HEAD_PAD
    ones_lane = lax.broadcasted_iota(jnp.int32, (TM_IN, HEAD_PAD), 1) == V_HEAD_DIM
    for hd in range(MLA_HEADS):
        lo = hd * HEAD_PAD
        q_ref[hd] = (qq[:, lo:lo + HEAD_PAD] * cosf + qq[:, nq + lo:nq + lo + HEAD_PAD] * sinf).astype(BF16)
        k_ref[hd] = (kv[:, lo:lo + HEAD_PAD] + kpe).astype(BF16)
        v_ref[hd] = jnp.where(ones_lane, 1.0, kv[:, nq + lo:nq + lo + HEAD_PAD]).astype(BF16)


def _rope_kernel(pos_ref, invf_ref, cos_ref, sin_ref):
    ang = pos_ref[...].astype(F32) * invf_ref[...]
    cos_ref[...] = jnp.cos(ang)
    sin_ref[...] = jnp.sin(ang)


def _rope_tables(positions):
    n_freq = QK_ROPE_DIM // 2
    per_row = LANES // n_freq
    rows = BATCH * SEQ // per_row
    inv_freq = ROPE_THETA ** (-jnp.arange(0, QK_ROPE_DIM, 2, dtype=F32) / QK_ROPE_DIM)
    pos = jnp.repeat(positions.reshape(rows, per_row), n_freq, axis=1)
    invf = jnp.tile(inv_freq, per_row).reshape(1, LANES)
    dense = jax.ShapeDtypeStruct((rows, LANES), F32)
    whole = lambda r: pl.BlockSpec((r, LANES), lambda: (0, 0))
    cos, sin = pl.pallas_call(
        _rope_kernel, out_shape=(dense, dense), in_specs=[whole(rows), whole(1)], out_specs=(whole(rows), whole(rows)),
        name="rope_tables",
    )(pos, invf)

    def on_rope_lanes(t, fill):
        t = t.reshape(BATCH, SEQ, n_freq)
        pad = lambda n: jnp.full((BATCH, SEQ, n), fill, F32)
        return jnp.concatenate([pad(QK_NOPE_DIM), t, t, pad(HEAD_PAD - QK_DIM)], axis=-1)

    return on_rope_lanes(cos, 1.0), on_rope_lanes(sin, 0.0)


def _inproj(x, cos, sin, w, layer):
    nt = SEQ // TM_IN
    tile = lambda n: pl.BlockSpec((None, TM_IN, n), lambda b, i: (b, i, 0))
    heads = lambda n: pl.BlockSpec((None, n, TM_IN, LANES), lambda b, i: (b, 0, i, 0))
    consts = [w["mix_norm"], w["wa"], w["ba"], w["wlat"], w["blat"], w["wc"], w["bc"],
              w["q_norm"], w["wuq"], w["kv_norm"], w["wukv"],
              w["lru_conv_w"], w["lru_conv_b"], w["wbd"], w["bgate"], w["lam"]]
    carry = pltpu.VMEM((LRU_PAD, LRU_WIDTH), F32)
    return pl.pallas_call(
        _inproj_kernel,
        out_shape=(jax.ShapeDtypeStruct((BATCH, SEQ, LRU_WIDTH), BF16),
                   jax.ShapeDtypeStruct((BATCH, MLA_HEADS, SEQ, HEAD_PAD), BF16),
                   jax.ShapeDtypeStruct((BATCH, MLA_HEADS, SEQ, HEAD_PAD), BF16),
                   jax.ShapeDtypeStruct((BATCH, MLA_HEADS, SEQ, HEAD_PAD), BF16),
                   jax.ShapeDtypeStruct((BATCH, SEQ, CONV_CH), F32)),
        grid=(BATCH, nt),
        in_specs=[tile(D_MODEL), tile(HEAD_PAD), tile(HEAD_PAD)] + [_layer_spec(c, layer) for c in consts],
        out_specs=(tile(LRU_WIDTH), heads(MLA_HEADS), heads(MLA_HEADS), heads(MLA_HEADS), tile(CONV_CH)),
        scratch_shapes=[carry, carry],
        compiler_params=_params("parallel", "arbitrary"),
        name="inproj",
    )(x, cos, sin, *consts)


def _attn_kernel(q_ref, k_ref, v_ref, o_ref, s_scr, mx_scr, acc_scr):
    qi = pl.program_id(1)
    groups = TK // LANES
    row = lax.broadcasted_iota(jnp.int32, (TQ, TK), 0)
    col = lax.broadcasted_iota(jnp.int32, (TQ, TK), 1)
    mx_scr[...] = jnp.full(mx_scr.shape, -jnp.inf, F32)

    def scores(j, diagonal):
        k0 = pl.multiple_of(j * TK, TK)
        for h in range(MLA_HEADS):
            s = lax.dot_general(q_ref[h], k_ref[h, pl.ds(k0, TK), :], (((1,), (1,)), ((), ())),
                                preferred_element_type=F32)
            if diagonal:
                s = jnp.where(col <= row, s, -jnp.inf)
            s_scr[h, j] = s
            mx = mx_scr[h]
            for g in range(groups):
                mx = jnp.maximum(mx, s[:, g * LANES:(g + 1) * LANES])
            mx_scr[h] = mx

    def chunkwise(n, fn):
        def four(t, carry):
            for u in range(4):
                fn(4 * t + u)
            return carry

        lax.fori_loop(0, n // 4, four, 0)
        base = (n // 4) * 4

        @pl.when(n % 4 >= 2)
        def _():
            fn(base)
            fn(base + 1)

        @pl.when(n % 2 == 1)
        def _():
            fn(n - 1)

    chunkwise(qi, lambda j: scores(j, False))
    scores(qi, True)

    for h in range(MLA_HEADS):
        mx_scr[h] = jnp.broadcast_to(jnp.max(mx_scr[h], axis=1, keepdims=True), (TQ, LANES))
    acc_scr[...] = jnp.zeros(acc_scr.shape, F32)

    def accumulate(j):
        k0 = pl.multiple_of(j * TK, TK)
        for h in range(MLA_HEADS):
            p = jnp.exp2(s_scr[h, j] - jnp.concatenate([mx_scr[h]] * groups, axis=1))
            acc_scr[h] += _dot(p.astype(BF16), v_ref[h, pl.ds(k0, TK), :])

    chunkwise(qi + 1, accumulate)

    outs = []
    for h in range(MLA_HEADS):
        acc = acc_scr[h]
        outs.append(acc[:, :V_HEAD_DIM] / acc[:, V_HEAD_DIM:V_HEAD_DIM + 1])
    o_ref[...] = jnp.concatenate(outs, axis=1).astype(BF16)


def _attention(q, k, v):
    assert TQ == TK
    stat = pltpu.VMEM((MLA_HEADS, TQ, LANES), F32)
    return pl.pallas_call(
        _attn_kernel,
        out_shape=jax.ShapeDtypeStruct((BATCH, SEQ, MLA_HEADS * V_HEAD_DIM), BF16),
        grid=(BATCH, SEQ // TQ),
        in_specs=[pl.BlockSpec((None, MLA_HEADS, TQ, HEAD_PAD), lambda b, i: (b, 0, i, 0)),
                  pl.BlockSpec((None, MLA_HEADS, SEQ, HEAD_PAD), lambda b, i: (b, 0, 0, 0)),
                  pl.BlockSpec((None, MLA_HEADS, SEQ, HEAD_PAD), lambda b, i: (b, 0, 0, 0))],
        out_specs=pl.BlockSpec((None, TQ, MLA_HEADS * V_HEAD_DIM), lambda b, i: (b, i, 0)),
        scratch_shapes=[pltpu.VMEM((MLA_HEADS, SEQ // TK, TQ, TK), F32), stat, stat],
        compiler_params=_params("parallel", "parallel"),
        name="attention",
    )(q, k, v)


def _outproj_kernel(x_ref, ya_ref, ob_ref, c_ref, halo_ref, ng_ref, wg_ref, bg_ref, dw_ref, db_ref, lng_ref, lnb_ref,
                    wa_ref, wb_ref, wc_ref, bc_ref, wo_ref, o_ref):
    x = x_ref[...]
    xn = _rms(x, ng_ref[...]).astype(BF16)

    def gate(b):
        lo = b * D_MODEL
        return jax.nn.sigmoid(_dot(xn, wg_ref[:, lo:lo + D_MODEL]) + bg_ref[:, lo:lo + D_MODEL])

    merged = gate(0) * _dot(ya_ref[...], wa_ref[...])
    merged = merged + gate(1) * _dot(ob_ref[...], wb_ref[...])
    halo = jnp.where(pl.program_id(1) == 0, 0.0, halo_ref[...])
    win = jnp.concatenate([halo, c_ref[...]], axis=0)
    acc = db_ref[...] + _causal_taps(win, dw_ref, CONV_WIDTH, CONV_PAD, TM_OUT)
    mu = jnp.mean(acc, axis=-1, keepdims=True)
    dlt = acc - mu
    var = jnp.mean(dlt * dlt, axis=-1, keepdims=True)
    yn = dlt * lax.rsqrt(var + NORM_EPS) * lng_ref[...] + lnb_ref[...]
    y_c = _dot(jax.nn.silu(yn).astype(BF16), wc_ref[...]) + bc_ref[...]
    merged = merged + gate(2) * y_c
    o_ref[...] = x + _dot(merged.astype(BF16), wo_ref[...])


def _outproj(x, ya, ob, c, w, layer):
    tile = lambda n: pl.BlockSpec((None, TM_OUT, n), lambda b, i: (b, i, 0))
    halo_blocks = TM_OUT // CONV_PAD
    halo = pl.BlockSpec((None, CONV_PAD, CONV_CH), lambda b, i: (b, jnp.maximum(i * halo_blocks - 1, 0), 0))
    consts = [w["mix_norm"], w["wg"], w["bg"], w["conv_dw_w"], w["conv_dw_b"], w["conv_ln_g"], w["conv_ln_b"],
              w["lru_w_out"], w["mla_w_o"], w["conv_w_out"], w["conv_b_out"], w["w_out"]]
    return pl.pallas_call(
        _outproj_kernel,
        out_shape=jax.ShapeDtypeStruct((BATCH, SEQ, D_MODEL), F32),
        grid=(BATCH, SEQ // TM_OUT),
        in_specs=[tile(D_MODEL), tile(LRU_WIDTH), tile(MLA_HEADS * V_HEAD_DIM), tile(CONV_CH), halo]
        + [_layer_spec(c_, layer) for c_ in consts],
        out_specs=tile(D_MODEL),
        compiler_params=_params("parallel", "parallel"),
        name="outproj",
    )(x, ya, ob, c, c, *consts)


def _prep(p):
    row = lambda v: v.reshape(DEPTH, 1, -1).astype(F32)
    o1, o2, o3 = IN_A, IN_A + IN_B, IN_A + IN_B + IN_C
    w_in, b_in = p["w_in"], p["b_in"]
    oq, okv, ope = o1, o1 + Q_LORA_RANK, o1 + Q_LORA_RANK + KV_LORA_RANK
    half = QK_ROPE_DIM // 2

    def rope_cols(m):
        z = lambda n: jnp.zeros(m.shape[:-1] + (n,), m.dtype)
        plain = jnp.concatenate([z(QK_NOPE_DIM), m, z(HEAD_PAD - QK_DIM)], axis=-1)
        rot = jnp.concatenate([z(QK_NOPE_DIM), -m[..., half:], m[..., :half], z(HEAD_PAD - QK_DIM)], axis=-1)
        return plain, rot

    wpe_plain, wpe_rot = rope_cols(w_in[..., ope:o2])
    bpe_plain, bpe_rot = rope_cols(b_in[..., ope:o2])

    w_uq = p["w_uq"].reshape(DEPTH, Q_LORA_RANK, MLA_HEADS, QK_DIM)
    zq = jnp.zeros((DEPTH, Q_LORA_RANK, MLA_HEADS, HEAD_PAD - QK_DIM), F32)
    q_plain = jnp.concatenate([w_uq, zq], axis=-1)
    q_pe = w_uq[..., QK_NOPE_DIM:]
    q_rot = jnp.concatenate([jnp.zeros((DEPTH, Q_LORA_RANK, MLA_HEADS, QK_NOPE_DIM), F32), -q_pe[..., half:],
                             q_pe[..., :half], zq], axis=-1)
    wuq = jnp.concatenate([q_plain.reshape(DEPTH, Q_LORA_RANK, -1), q_rot.reshape(DEPTH, Q_LORA_RANK, -1)], axis=-1)

    w_ukv = p["w_ukv"].reshape(DEPTH, KV_LORA_RANK, MLA_HEADS, QK_NOPE_DIM + V_HEAD_DIM)
    wk = jnp.concatenate([w_ukv[..., :QK_NOPE_DIM],
                          jnp.zeros((DEPTH, KV_LORA_RANK, MLA_HEADS, HEAD_PAD - QK_NOPE_DIM), F32)], axis=-1)
    wv = jnp.concatenate([w_ukv[..., QK_NOPE_DIM:],
                          jnp.zeros((DEPTH, KV_LORA_RANK, MLA_HEADS, HEAD_PAD - V_HEAD_DIM), F32)], axis=-1)

    wg4 = p["lru_w_gate"]
    eye = jnp.eye(LRU_HEADS, dtype=F32)
    bd = lambda blk: jnp.einsum("lhde,hg->lhdge", blk, eye).reshape(DEPTH, LRU_WIDTH, LRU_WIDTH)
    wbd = jnp.concatenate([bd(wg4[..., :LRU_HEAD_DIM]), bd(wg4[..., LRU_HEAD_DIM:])], axis=-1)
    bgate = jnp.concatenate([p["lru_b_gate"][..., :LRU_HEAD_DIM].reshape(DEPTH, -1),
                             p["lru_b_gate"][..., LRU_HEAD_DIM:].reshape(DEPTH, -1)], axis=-1)

    return dict(
        ffn1_norm=row(p["ffn1_norm"]), ffn2_norm=row(p["ffn2_norm"]), mix_norm=row(p["mix_norm"]),
        wa=w_in[..., :o1].astype(BF16), ba=row(b_in[..., :o1]),
        wlat=jnp.concatenate([w_in[..., oq:ope], wpe_plain, wpe_rot], axis=-1).astype(BF16),
        blat=row(jnp.concatenate([b_in[..., oq:ope], bpe_plain, bpe_rot], axis=-1)),
        wc=w_in[..., o2:o3].astype(BF16), bc=row(b_in[..., o2:o3]),
        wg=w_in[..., o3:].astype(BF16), bg=row(b_in[..., o3:]),
        q_norm=row(p["q_norm"]), wuq=(wuq * (QK_DIM ** -0.5 * LOG2_E)).astype(BF16), kv_norm=row(p["kv_norm"]),
        wukv=jnp.concatenate([wk.reshape(DEPTH, KV_LORA_RANK, -1), wv.reshape(DEPTH, KV_LORA_RANK, -1)],
                             axis=-1).astype(BF16),
        lru_conv_w=p["lru_conv_w"].astype(F32), lru_conv_b=row(p["lru_conv_b"]),
        wbd=wbd.astype(BF16), bgate=row(bgate), lam=row(p["lru_lambda"]),
        conv_dw_w=p["conv_dw_w"].astype(F32), conv_dw_b=row(p["conv_dw_b"]),
        conv_ln_g=row(p["conv_ln_g"]), conv_ln_b=row(p["conv_ln_b"]),
        lru_w_out=p["lru_w_out"].astype(BF16), mla_w_o=p["mla_w_o"].astype(BF16),
        conv_w_out=p["conv_w_out"].astype(BF16), conv_b_out=row(p["conv_b_out"]), w_out=p["w_out"].astype(BF16),
    )


def kernel(x, positions, ffn1_norm, ffn1_w1, ffn1_w2, mix_norm, w_in, b_in, lru_conv_w, lru_conv_b, lru_w_gate, lru_b_gate, lru_lambda, lru_w_out, q_norm, w_uq, kv_norm, w_ukv, mla_w_o, conv_dw_w, conv_dw_b, conv_ln_g, conv_ln_b, conv_w_out, conv_b_out, w_out, ffn2_norm, ffn2_w1, ffn2_w2, final_norm):
    stacked = dict(ffn1_norm=ffn1_norm, mix_norm=mix_norm, w_in=w_in, b_in=b_in,
                   lru_conv_w=lru_conv_w, lru_conv_b=lru_conv_b, lru_w_gate=lru_w_gate, lru_b_gate=lru_b_gate,
                   lru_lambda=lru_lambda, lru_w_out=lru_w_out, q_norm=q_norm, w_uq=w_uq, kv_norm=kv_norm,
                   w_ukv=w_ukv, mla_w_o=mla_w_o, conv_dw_w=conv_dw_w, conv_dw_b=conv_dw_b, conv_ln_g=conv_ln_g,
                   conv_ln_b=conv_ln_b, conv_w_out=conv_w_out, conv_b_out=conv_b_out, w_out=w_out,
                   ffn2_norm=ffn2_norm)
    tok = BATCH * SEQ
    final_g = final_norm.reshape(1, 1, D_MODEL).astype(F32)
    xf = x.reshape(tok, D_MODEL)
    w = _prep(stacked)
    cos, sin = _rope_tables(positions)
    for l in range(DEPTH):
        xf = _ffn(xf, w["ffn1_norm"], ffn1_w1, ffn1_w2, l)
        xb = xf.reshape(BATCH, SEQ, D_MODEL)
        ya, q, k, v, c = _inproj(xb, cos, sin, w, l)
        ob = _attention(q, k, v)
        xf = _outproj(xb, ya, ob, c, w, l).reshape(tok, D_MODEL)
        xf = _ffn(xf, w["ffn2_norm"], ffn2_w1, ffn2_w2, l, final_g if l == DEPTH - 1 else None)
    return xf.reshape(BATCH, SEQ, D_MODEL)
```

```python
import functools

import jax
import jax.numpy as jnp
from jax import lax
from jax.experimental import pallas as pl
from jax.experimental.pallas import tpu as pltpu

D_MODEL = 1024
BATCH = 8
SEQ = 2048
DEPTH = 2
D_FF = 2816
NORM_EPS = 1e-6
LRU_WIDTH = 512
LRU_HEADS = 8
LRU_HEAD_DIM = LRU_WIDTH // LRU_HEADS
LRU_CONV_WIDTH = 4
LRU_C = 8.0
MLA_HEADS = 8
QK_NOPE_DIM = 64
QK_ROPE_DIM = 32
V_HEAD_DIM = 64
Q_LORA_RANK = 384
KV_LORA_RANK = 256
ROPE_THETA = 10000.0
CONV_CH = 512
CONV_WIDTH = 31
N_BRANCH = 3
IN_A = 2 * LRU_WIDTH
IN_B = Q_LORA_RANK + KV_LORA_RANK + QK_ROPE_DIM
IN_C = 2 * CONV_CH
IN_G = N_BRANCH * D_MODEL

LANES = 128
SUBLANES = 8
HEAD_PAD = LANES
QK_DIM = QK_NOPE_DIM + QK_ROPE_DIM
LOG2_E = 1.4426950408889634
VMEM_LIMIT = 56 * 1024 * 1024

TM_FFN = 512
FF_CHUNK = 256
TM_IN = 1024
TM_OUT = 512
CONV_PAD = 32
LRU_PAD = SUBLANES
TQ = 256
TK = 256

F32 = jnp.float32
BF16 = jnp.bfloat16


def _layer_spec(stacked, layer):
    nd = stacked.ndim - 1
    return pl.BlockSpec((None,) + stacked.shape[1:], lambda *_: (layer,) + (0,) * nd, pipeline_mode=pl.Buffered(1))


def _params(*semantics):
    return pltpu.CompilerParams(dimension_semantics=semantics, vmem_limit_bytes=VMEM_LIMIT)


def _rms(x, g):
    return x * lax.rsqrt(jnp.mean(x * x, axis=-1, keepdims=True) + NORM_EPS) * g


def _dot(a, b):
    return jnp.dot(a, b, preferred_element_type=F32)


def _ffn_slab_copies(w1_hbm, w2_hbm, layer, c, stages, sem):
    slot, lo = c % 2, c * FF_CHUNK
    srcs = (w1_hbm.at[layer, :, pl.ds(lo, FF_CHUNK)], w1_hbm.at[layer, :, pl.ds(D_FF + lo, FF_CHUNK)],
            w2_hbm.at[layer, pl.ds(lo, FF_CHUNK), :])
    return [pltpu.make_async_copy(src, stage.at[slot], sem.at[n, slot])
            for n, (src, stage) in enumerate(zip(srcs, stages))]


def _ffn_kernel(x_ref, g_ref, w1_hbm, w2_hbm, *rest, layer, final):
    if final:
        fg_ref, o_ref, w1_ref, w2_ref, stage_g, stage_u, stage_d, sem = rest
    else:
        o_ref, w1_ref, w2_ref, stage_g, stage_u, stage_d, sem = rest
    stages = (stage_g, stage_u, stage_d)
    n_chunks = D_FF // FF_CHUNK
    copies = lambda c: _ffn_slab_copies(w1_hbm, w2_hbm, layer, c, stages, sem)

    def tile(load_weights):
        x = x_ref[...]
        xn = _rms(x, g_ref[...]).astype(BF16)
        if load_weights:
            for cp in copies(0):
                cp.start()
        acc = None
        for c in range(n_chunks):
            lo = c * FF_CHUNK
            if load_weights:
                if c + 1 < n_chunks:
                    for cp in copies(c + 1):
                        cp.start()
                for cp in copies(c):
                    cp.wait()
                w1_ref[:, lo:lo + FF_CHUNK] = stage_g[c % 2].astype(BF16)
                w1_ref[:, D_FF + lo:D_FF + lo + FF_CHUNK] = stage_u[c % 2].astype(BF16)
                w2_ref[lo:lo + FF_CHUNK, :] = stage_d[c % 2].astype(BF16)
            g = _dot(xn, w1_ref[:, lo:lo + FF_CHUNK])
            u = _dot(xn, w1_ref[:, D_FF + lo:D_FF + lo + FF_CHUNK])
            h = (jax.nn.silu(g) * u).astype(BF16)
            d = _dot(h, w2_ref[lo:lo + FF_CHUNK, :])
            acc = d if acc is None else acc + d
        y = x + 0.5 * acc
        if final:
            y = _rms(y, fg_ref[...])
        o_ref[...] = y

    first_step = pl.program_id(0) == 0

    @pl.when(first_step)
    def _():
        tile(True)

    @pl.when(jnp.logical_not(first_step))
    def _():
        tile(False)


def _ffn(x, norm_g, w1_stack, w2_stack, layer, final_g=None):
    tok = x.shape[0]
    final = final_g is not None
    hbm = pl.BlockSpec(memory_space=pl.ANY)
    in_specs = [pl.BlockSpec((TM_FFN, D_MODEL), lambda i: (i, 0)), _layer_spec(norm_g, layer), hbm, hbm]
    args = [x, norm_g, w1_stack, w2_stack]
    if final:
        in_specs.append(_layer_spec(final_g, 0))
        args.append(final_g)
    return pl.pallas_call(
        functools.partial(_ffn_kernel, layer=layer, final=final),
        out_shape=jax.ShapeDtypeStruct((tok, D_MODEL), F32),
        grid=(tok // TM_FFN,),
        in_specs=in_specs,
        out_specs=pl.BlockSpec((TM_FFN, D_MODEL), lambda i: (i, 0)),
        scratch_shapes=[pltpu.VMEM((D_MODEL, 2 * D_FF), BF16), pltpu.VMEM((D_FF, D_MODEL), BF16),
                        pltpu.VMEM((2, D_MODEL, FF_CHUNK), F32), pltpu.VMEM((2, D_MODEL, FF_CHUNK), F32),
                        pltpu.VMEM((2, FF_CHUNK, D_MODEL), F32), pltpu.SemaphoreType.DMA((3, 2))],
        compiler_params=_params("arbitrary"),
        name="ffn_final" if final else "ffn",
    )(*args)


def _causal_taps(win, w_ref, n_taps, pad, rows):
    n = win.shape[0]
    offs = [pad - (n_taps - 1) + j for j in range(n_taps)]
    acc = None
    for r in range(SUBLANES):
        taps = [j for j in range(n_taps) if offs[j] % SUBLANES == r]
        if not taps:
            continue
        shifted = win if r == 0 else pltpu.roll(win, n - r, axis=0)
        for j in taps:
            lo = offs[j] - r
            term = w_ref[j:j + 1, :] * shifted[lo:lo + rows, :]
            acc = term if acc is None else acc + term
    return acc


def _linear_scan(a, u, h0):
    rows, width = a.shape
    row = lax.broadcasted_iota(jnp.int32, (SUBLANES, width), 0)
    keeps = [(d, row >= d) for d in (1, 2, 4)]
    h, out = h0, []
    for k in range(rows // SUBLANES):
        av = a[k * SUBLANES:(k + 1) * SUBLANES, :]
        hv = u[k * SUBLANES:(k + 1) * SUBLANES, :]
        for d, keep in keeps:
            hv = jnp.where(keep, av * pltpu.roll(hv, d, axis=0) + hv, hv)
            av = jnp.where(keep, av * pltpu.roll(av, d, axis=0), av)
        hv = hv + av * h
        out.append(hv)
        h = jnp.broadcast_to(hv[SUBLANES - 1:SUBLANES, :], (SUBLANES, width))
    return jnp.concatenate(out, axis=0), h


def _inproj_kernel(x_ref, cos_ref, sin_ref, ng_ref, wa_ref, ba_ref, wlat_ref, blat_ref, wc_ref, bc_ref,
                   qn_ref, wuq_ref, kvn_ref, wukv_ref, cw_ref, cb_ref, wbd_ref, bgate_ref, lam_ref,
                   ya_ref, q_ref, k_ref, v_ref, c_ref, tail_scr, h_scr):
    @pl.when(pl.program_id(1) == 0)
    def _():
        tail_scr[...] = jnp.zeros(tail_scr.shape, F32)
        h_scr[...] = jnp.zeros(h_scr.shape, F32)

    xn = _rms(x_ref[...], ng_ref[...]).astype(BF16)
    pa = _dot(xn, wa_ref[...]) + ba_ref[...]
    xa_pre = pa[:, :LRU_WIDTH]
    win = jnp.concatenate([tail_scr[...], xa_pre], axis=0)
    tail_scr[...] = xa_pre[TM_IN - LRU_PAD:, :]
    xa = cb_ref[...] + _causal_taps(win, cw_ref, LRU_CONV_WIDTH, LRU_PAD, TM_IN)
    gates = _dot(xa.astype(BF16), wbd_ref[...]) + bgate_ref[...]
    r = jax.nn.sigmoid(gates[:, :LRU_WIDTH])
    ig = jax.nn.sigmoid(gates[:, LRU_WIDTH:])
    log_a = (-LRU_C) * r * jax.nn.softplus(-lam_ref[...])
    a = jnp.exp(log_a)
    one_m = (1.0 + a * a) * jnp.tanh(-log_a)
    root = jnp.where(one_m > 0.0, one_m * lax.rsqrt(one_m), 0.0)
    h, h_scr[...] = _linear_scan(a, root * (ig * xa), h_scr[...])
    ya_ref[...] = (h * jax.nn.gelu(pa[:, LRU_WIDTH:])).astype(BF16)
    pc = _dot(xn, wc_ref[...]) + bc_ref[...]
    c_ref[...] = pc[:, :CONV_CH] * jax.nn.sigmoid(pc[:, CONV_CH:])
    lat = _dot(xn, wlat_ref[...]) + blat_ref[...]
    okv, ope = Q_LORA_RANK, Q_LORA_RANK + KV_LORA_RANK
    qq = _dot(_rms(lat[:, :okv], qn_ref[...]).astype(BF16), wuq_ref[...])
    kv = _dot(_rms(lat[:, okv:ope], kvn_ref[...]).astype(BF16), wukv_ref[...])
    cosf, sinf = cos_ref[...], sin_ref[...]
    kpe = lat[:, ope:ope + HEAD_PAD] * cosf + lat[:, ope + HEAD_PAD:] * sinf
    nq = MLA_HEADS * HEAD_PAD
    ones_lane = lax.broadcasted_iota(jnp.int32, (TM_IN, HEAD_PAD), 1) == V_HEAD_DIM
    for hd in range(MLA_HEADS):
        lo = hd * HEAD_PAD
        q_ref[hd] = (qq[:, lo:lo + HEAD_PAD] * cosf + qq[:, nq + lo:nq + lo + HEAD_PAD] * sinf).astype(BF16)
        k_ref[hd] = (kv[:, lo:lo + HEAD_PAD] + kpe).astype(BF16)
        v_ref[hd] = jnp.where(ones_lane, 1.0, kv[:, nq + lo:nq + lo + HEAD_PAD]).astype(BF16)


def _rope_kernel(pos_ref, invf_ref, cos_ref, sin_ref):
    ang = pos_ref[...].astype(F32) * invf_ref[...]
    cos_ref[...] = jnp.cos(ang)
    sin_ref[...] = jnp.sin(ang)


def _rope_tables(positions):
    n_freq = QK_ROPE_DIM // 2
    per_row = LANES // n_freq
    rows = BATCH * SEQ // per_row
    inv_freq = ROPE_THETA ** (-jnp.arange(0, QK_ROPE_DIM, 2, dtype=F32) / QK_ROPE_DIM)
    pos = jnp.repeat(positions.reshape(rows, per_row), n_freq, axis=1)
    invf = jnp.tile(inv_freq, per_row).reshape(1, LANES)
    dense = jax.ShapeDtypeStruct((rows, LANES), F32)
    whole = lambda r: pl.BlockSpec((r, LANES), lambda: (0, 0))
    cos, sin = pl.pallas_call(
        _rope_kernel, out_shape=(dense, dense), in_specs=[whole(rows), whole(1)], out_specs=(whole(rows), whole(rows)),
        name="rope_tables",
    )(pos, invf)

    def on_rope_lanes(t, fill):
        t = t.reshape(BATCH, SEQ, n_freq)
        pad = lambda n: jnp.full((BATCH, SEQ, n), fill, F32)
        return jnp.concatenate([pad(QK_NOPE_DIM), t, t, pad(HEAD_PAD - QK_DIM)], axis=-1)

    return on_rope_lanes(cos, 1.0), on_rope_lanes(sin, 0.0)


def _inproj(x, cos, sin, w, layer):
    nt = SEQ // TM_IN
    tile = lambda n: pl.BlockSpec((None, TM_IN, n), lambda b, i: (b, i, 0))
    heads = lambda n: pl.BlockSpec((None, n, TM_IN, LANES), lambda b, i: (b, 0, i, 0))
    consts = [w["mix_norm"], w["wa"], w["ba"], w["wlat"], w["blat"], w["wc"], w["bc"],
              w["q_norm"], w["wuq"], w["kv_norm"], w["wukv"],
              w["lru_conv_w"], w["lru_conv_b"], w["wbd"], w["bgate"], w["lam"]]
    carry = pltpu.VMEM((LRU_PAD, LRU_WIDTH), F32)
    return pl.pallas_call(
        _inproj_kernel,
        out_shape=(jax.ShapeDtypeStruct((BATCH, SEQ, LRU_WIDTH), BF16),
                   jax.ShapeDtypeStruct((BATCH, MLA_HEADS, SEQ, HEAD_PAD), BF16),
                   jax.ShapeDtypeStruct((BATCH, MLA_HEADS, SEQ, HEAD_PAD), BF16),
                   jax.ShapeDtypeStruct((BATCH, MLA_HEADS, SEQ, HEAD_PAD), BF16),
                   jax.ShapeDtypeStruct((BATCH, SEQ, CONV_CH), F32)),
        grid=(BATCH, nt),
        in_specs=[tile(D_MODEL), tile(HEAD_PAD), tile(HEAD_PAD)] + [_layer_spec(c, layer) for c in consts],
        out_specs=(tile(LRU_WIDTH), heads(MLA_HEADS), heads(MLA_HEADS), heads(MLA_HEADS), tile(CONV_CH)),
        scratch_shapes=[carry, carry],
        compiler_params=_params("parallel", "arbitrary"),
        name="inproj",
    )(x, cos, sin, *consts)


def _attn_kernel(q_ref, k_ref, v_ref, o_ref, s_scr, mx_scr, acc_scr):
    qi = pl.program_id(1)
    groups = TK // LANES
    row = lax.broadcasted_iota(jnp.int32, (TQ, TK), 0)
    col = lax.broadcasted_iota(jnp.int32, (TQ, TK), 1)
    mx_scr[...] = jnp.full(mx_scr.shape, -jnp.inf, F32)

    def scores(j, diagonal):
        k0 = pl.multiple_of(j * TK, TK)
        for h in range(MLA_HEADS):
            s = lax.dot_general(q_ref[h], k_ref[h, pl.ds(k0, TK), :], (((1,), (1,)), ((), ())),
                                preferred_element_type=F32)
            if diagonal:
                s = jnp.where(col <= row, s, -jnp.inf)
            s_scr[h, j] = s
            mx = mx_scr[h]
            for g in range(groups):
                mx = jnp.maximum(mx, s[:, g * LANES:(g + 1) * LANES])
            mx_scr[h] = mx

    def chunkwise(n, fn):
        def four(t, carry):
            for u in range(4):
                fn(4 * t + u)
            return carry

        lax.fori_loop(0, n // 4, four, 0)
        base = (n // 4) * 4

        @pl.when(n % 4 >= 2)
        def _():
            fn(base)
            fn(base + 1)

        @pl.when(n % 2 == 1)
        def _():
            fn(n - 1)

    chunkwise(qi, lambda j: scores(j, False))
    scores(qi, True)

    for h in range(MLA_HEADS):
        mx_scr[h] = jnp.broadcast_to(jnp.max(mx_scr[h], axis=1, keepdims=True), (TQ, LANES))
    acc_scr[...] = jnp.zeros(acc_scr.shape, F32)

    def accumulate(j):
        k0 = pl.multiple_of(j * TK, TK)
        for h in range(MLA_HEADS):
            p = jnp.exp2(s_scr[h, j] - jnp.concatenate([mx_scr[h]] * groups, axis=1))
            acc_scr[h] += _dot(p.astype(BF16), v_ref[h, pl.ds(k0, TK), :])

    chunkwise(qi + 1, accumulate)

    outs = []
    for h in range(MLA_HEADS):
        acc = acc_scr[h]
        outs.append(acc[:, :V_HEAD_DIM] / acc[:, V_HEAD_DIM:V_HEAD_DIM + 1])
    o_ref[...] = jnp.concatenate(outs, axis=1).astype(BF16)


def _attention(q, k, v):
    assert TQ == TK
    stat = pltpu.VMEM((MLA_HEADS, TQ, LANES), F32)
    return pl.pallas_call(
        _attn_kernel,
        out_shape=jax.ShapeDtypeStruct((BATCH, SEQ, MLA_HEADS * V_HEAD_DIM), BF16),
        grid=(BATCH, SEQ // TQ),
        in_specs=[pl.BlockSpec((None, MLA_HEADS, TQ, HEAD_PAD), lambda b, i: (b, 0, i, 0)),
                  pl.BlockSpec((None, MLA_HEADS, SEQ, HEAD_PAD), lambda b, i: (b, 0, 0, 0)),
                  pl.BlockSpec((None, MLA_HEADS, SEQ, HEAD_PAD), lambda b, i: (b, 0, 0, 0))],
        out_specs=pl.BlockSpec((None, TQ, MLA_HEADS * V_HEAD_DIM), lambda b, i: (b, i, 0)),
        scratch_shapes=[pltpu.VMEM((MLA_HEADS, SEQ // TK, TQ, TK), F32), stat, stat],
        compiler_params=_params("parallel", "parallel"),
        name="attention",
    )(q, k, v)


def _outproj_kernel(x_ref, ya_ref, ob_ref, c_ref, halo_ref, ng_ref, wg_ref, bg_ref, dw_ref, db_ref, lng_ref, lnb_ref,
                    wa_ref, wb_ref, wc_ref, bc_ref, wo_ref, o_ref):
    x = x_ref[...]
    xn = _rms(x, ng_ref[...]).astype(BF16)

    def gate(b):
        lo = b * D_MODEL
        return jax.nn.sigmoid(_dot(xn, wg_ref[:, lo:lo + D_MODEL]) + bg_ref[:, lo:lo + D_MODEL])

    merged = gate(0) * _dot(ya_ref[...], wa_ref[...])
    merged = merged + gate(1) * _dot(ob_ref[...], wb_ref[...])
    halo = jnp.where(pl.program_id(1) == 0, 0.0, halo_ref[...])
    win = jnp.concatenate([halo, c_ref[...]], axis=0)
    acc = db_ref[...] + _causal_taps(win, dw_ref, CONV_WIDTH, CONV_PAD, TM_OUT)
    mu = jnp.mean(acc, axis=-1, keepdims=True)
    dlt = acc - mu
    var = jnp.mean(dlt * dlt, axis=-1, keepdims=True)
    yn = dlt * lax.rsqrt(var + NORM_EPS) * lng_ref[...] + lnb_ref[...]
    y_c = _dot(jax.nn.silu(yn).astype(BF16), wc_ref[...]) + bc_ref[...]
    merged = merged + gate(2) * y_c
    o_ref[...] = x + _dot(merged.astype(BF16), wo_ref[...])


def _outproj(x, ya, ob, c, w, layer):
    tile = lambda n: pl.BlockSpec((None, TM_OUT, n), lambda b, i: (b, i, 0))
    halo_blocks = TM_OUT // CONV_PAD
    halo = pl.BlockSpec((None, CONV_PAD, CONV_CH), lambda b, i: (b, jnp.maximum(i * halo_blocks - 1, 0), 0))
    consts = [w["mix_norm"], w["wg"], w["bg"], w["conv_dw_w"], w["conv_dw_b"], w["conv_ln_g"], w["conv_ln_b"],
              w["lru_w_out"], w["mla_w_o"], w["conv_w_out"], w["conv_b_out"], w["w_out"]]
    return pl.pallas_call(
        _outproj_kernel,
        out_shape=jax.ShapeDtypeStruct((BATCH, SEQ, D_MODEL), F32),
        grid=(BATCH, SEQ // TM_OUT),
        in_specs=[tile(D_MODEL), tile(LRU_WIDTH), tile(MLA_HEADS * V_HEAD_DIM), tile(CONV_CH), halo]
        + [_layer_spec(c_, layer) for c_ in consts],
        out_specs=tile(D_MODEL),
        compiler_params=_params("parallel", "parallel"),
        name="outproj",
    )(x, ya, ob, c, c, *consts)


def _prep(p):
    row = lambda v: v.reshape(DEPTH, 1, -1).astype(F32)
    o1, o2, o3 = IN_A, IN_A + IN_B, IN_A + IN_B + IN_C
    w_in, b_in = p["w_in"], p["b_in"]
    oq, okv, ope = o1, o1 + Q_LORA_RANK, o1 + Q_LORA_RANK + KV_LORA_RANK
    half = QK_ROPE_DIM // 2

    def rope_cols(m):
        z = lambda n: jnp.zeros(m.shape[:-1] + (n,), m.dtype)
        plain = jnp.concatenate([z(QK_NOPE_DIM), m, z(HEAD_PAD - QK_DIM)], axis=-1)
        rot = jnp.concatenate([z(QK_NOPE_DIM), -m[..., half:], m[..., :half], z(HEAD_PAD - QK_DIM)], axis=-1)
        return plain, rot

    wpe_plain, wpe_rot = rope_cols(w_in[..., ope:o2])
    bpe_plain, bpe_rot = rope_cols(b_in[..., ope:o2])

    w_uq = p["w_uq"].reshape(DEPTH, Q_LORA_RANK, MLA_HEADS, QK_DIM)
    zq = jnp.zeros((DEPTH, Q_LORA_RANK, MLA_HEADS, HEAD_PAD - QK_DIM), F32)
    q_plain = jnp.concatenate([w_uq, zq], axis=-1)
    q_pe = w_uq[..., QK_NOPE_DIM:]
    q_rot = jnp.concatenate([jnp.zeros((DEPTH, Q_LORA_RANK, MLA_HEADS, QK_NOPE_DIM), F32), -q_pe[..., half:],
                             q_pe[..., :half], zq], axis=-1)
    wuq = jnp.concatenate([q_plain.reshape(DEPTH, Q_LORA_RANK, -1), q_rot.reshape(DEPTH, Q_LORA_RANK, -1)], axis=-1)

    w_ukv = p["w_ukv"].reshape(DEPTH, KV_LORA_RANK, MLA_HEADS, QK_NOPE_DIM + V_HEAD_DIM)
    wk = jnp.concatenate([w_ukv[..., :QK_NOPE_DIM],
                          jnp.zeros((DEPTH, KV_LORA_RANK, MLA_HEADS, HEAD_PAD - QK_NOPE_DIM), F32)], axis=-1)
    wv = jnp.concatenate([w_ukv[..., QK_NOPE_DIM:],
                          jnp.zeros((DEPTH, KV_LORA_RANK, MLA_HEADS, HEAD_PAD - V_HEAD_DIM), F32)], axis=-1)

    wg4 = p["lru_w_gate"]
    eye = jnp.eye(LRU_HEADS, dtype=F32)
    bd = lambda blk: jnp.einsum("lhde,hg->lhdge", blk, eye).reshape(DEPTH, LRU_WIDTH, LRU_WIDTH)
    wbd = jnp.concatenate([bd(wg4[..., :LRU_HEAD_DIM]), bd(wg4[..., LRU_HEAD_DIM:])], axis=-1)
    bgate = jnp.concatenate([p["lru_b_gate"][..., :LRU_HEAD_DIM].reshape(DEPTH, -1),
                             p["lru_b_gate"][..., LRU_HEAD_DIM:].reshape(DEPTH, -1)], axis=-1)

    return dict(
        ffn1_norm=row(p["ffn1_norm"]), ffn2_norm=row(p["ffn2_norm"]), mix_norm=row(p["mix_norm"]),
        wa=w_in[..., :o1].astype(BF16), ba=row(b_in[..., :o1]),
        wlat=jnp.concatenate([w_in[..., oq:ope], wpe_plain, wpe_rot], axis=-1).astype(BF16),
        blat=row(jnp.concatenate([b_in[..., oq:ope], bpe_plain, bpe_rot], axis=-1)),
        wc=w_in[..., o2:o3].astype(BF16), bc=row(b_in[..., o2:o3]),
        wg=w_in[..., o3:].astype(BF16), bg=row(b_in[..., o3:]),
        q_norm=row(p["q_norm"]), wuq=(wuq * (QK_DIM ** -0.5 * LOG2_E)).astype(BF16), kv_norm=row(p["kv_norm"]),
        wukv=jnp.concatenate([wk.reshape(DEPTH, KV_LORA_RANK, -1), wv.reshape(DEPTH, KV_LORA_RANK, -1)],
                             axis=-1).astype(BF16),
        lru_conv_w=p["lru_conv_w"].astype(F32), lru_conv_b=row(p["lru_conv_b"]),
        wbd=wbd.astype(BF16), bgate=row(bgate), lam=row(p["lru_lambda"]),
        conv_dw_w=p["conv_dw_w"].astype(F32), conv_dw_b=row(p["conv_dw_b"]),
        conv_ln_g=row(p["conv_ln_g"]), conv_ln_b=row(p["conv_ln_b"]),
        lru_w_out=p["lru_w_out"].astype(BF16), mla_w_o=p["mla_w_o"].astype(BF16),
        conv_w_out=p["conv_w_out"].astype(BF16), conv_b_out=row(p["conv_b_out"]), w_out=p["w_out"].astype(BF16),
    )


def kernel(x, positions, ffn1_norm, ffn1_w1, ffn1_w2, mix_norm, w_in, b_in, lru_conv_w, lru_conv_b, lru_w_gate, lru_b_gate, lru_lambda, lru_w_out, q_norm, w_uq, kv_norm, w_ukv, mla_w_o, conv_dw_w, conv_dw_b, conv_ln_g, conv_ln_b, conv_w_out, conv_b_out, w_out, ffn2_norm, ffn2_w1, ffn2_w2, final_norm):
    stacked = dict(ffn1_norm=ffn1_norm, mix_norm=mix_norm, w_in=w_in, b_in=b_in,
                   lru_conv_w=lru_conv_w, lru_conv_b=lru_conv_b, lru_w_gate=lru_w_gate, lru_b_gate=lru_b_gate,
                   lru_lambda=lru_lambda, lru_w_out=lru_w_out, q_norm=q_norm, w_uq=w_uq, kv_norm=kv_norm,
                   w_ukv=w_ukv, mla_w_o=mla_w_o, conv_dw_w=conv_dw_w, conv_dw_b=conv_dw_b, conv_ln_g=conv_ln_g,
                   conv_ln_b=conv_ln_b, conv_w_out=conv_w_out, conv_b_out=conv_b_out, w_out=w_out,
                   ffn2_norm=ffn2_norm)
    tok = BATCH * SEQ
    final_g = final_norm.reshape(1, 1, D_MODEL).astype(F32)
    xf = x.reshape(tok, D_MODEL)
    w = _prep(stacked)
    cos, sin = _rope_tables(positions)
    for l in range(DEPTH):
        xf = _ffn(xf, w["ffn1_norm"], ffn1_w1, ffn1_w2, l)
        xb = xf.reshape(BATCH, SEQ, D_MODEL)
        ya, q, k, v, c = _inproj(xb, cos, sin, w, l)
        ob = _attention(q, k, v)
        xf = _outproj(xb, ya, ob, c, w, l).reshape(tok, D_MODEL)
        xf = _ffn(xf, w["ffn2_norm"], ffn2_w1, ffn2_w2, l, final_g if l == DEPTH - 1 else None)
    return xf.reshape(BATCH, SEQ, D_MODEL)
```

```python
import functools

import jax
import jax.numpy as jnp
from jax import lax
from jax.experimental import pallas as pl
from jax.experimental.pallas import tpu as pltpu

D_MODEL = 1024
BATCH = 8
SEQ = 2048
DEPTH = 2
D_FF = 2816
NORM_EPS = 1e-6
LRU_WIDTH = 512
LRU_HEADS = 8
LRU_HEAD_DIM = LRU_WIDTH // LRU_HEADS
LRU_CONV_WIDTH = 4
LRU_C = 8.0
MLA_HEADS = 8
QK_NOPE_DIM = 64
QK_ROPE_DIM = 32
V_HEAD_DIM = 64
Q_LORA_RANK = 384
KV_LORA_RANK = 256
ROPE_THETA = 10000.0
CONV_CH = 512
CONV_WIDTH = 31
N_BRANCH = 3
IN_A = 2 * LRU_WIDTH
IN_B = Q_LORA_RANK + KV_LORA_RANK + QK_ROPE_DIM
IN_C = 2 * CONV_CH
IN_G = N_BRANCH * D_MODEL

LANES = 128
SUBLANES = 8
HEAD_PAD = LANES
QK_DIM = QK_NOPE_DIM + QK_ROPE_DIM
LOG2_E = 1.4426950408889634
VMEM_LIMIT = 56 * 1024 * 1024

TM_FFN = 512
FF_CHUNK = 256
TM_IN = 1024
TM_OUT = 1024
CONV_PAD = 32
LRU_PAD = SUBLANES
TQ = 256
TK = 256

F32 = jnp.float32
BF16 = jnp.bfloat16


def _layer_spec(stacked, layer):
    nd = stacked.ndim - 1
    return pl.BlockSpec((None,) + stacked.shape[1:], lambda *_: (layer,) + (0,) * nd, pipeline_mode=pl.Buffered(1))


def _params(*semantics):
    return pltpu.CompilerParams(dimension_semantics=semantics, vmem_limit_bytes=VMEM_LIMIT)


def _rms(x, g):
    return x * lax.rsqrt(jnp.mean(x * x, axis=-1, keepdims=True) + NORM_EPS) * g


def _dot(a, b):
    return jnp.dot(a, b, preferred_element_type=F32)


def _ffn_slab_copies(w1_hbm, w2_hbm, layer, c, stages, sem):
    slot, lo = c % 2, c * FF_CHUNK
    srcs = (w1_hbm.at[layer, :, pl.ds(lo, FF_CHUNK)], w1_hbm.at[layer, :, pl.ds(D_FF + lo, FF_CHUNK)],
            w2_hbm.at[layer, pl.ds(lo, FF_CHUNK), :])
    return [pltpu.make_async_copy(src, stage.at[slot], sem.at[n, slot])
            for n, (src, stage) in enumerate(zip(srcs, stages))]


def _ffn_kernel(x_ref, g_ref, w1_hbm, w2_hbm, *rest, layer, final):
    if final:
        fg_ref, o_ref, w1_ref, w2_ref, stage_g, stage_u, stage_d, sem = rest
    else:
        o_ref, w1_ref, w2_ref, stage_g, stage_u, stage_d, sem = rest
    stages = (stage_g, stage_u, stage_d)
    n_chunks = D_FF // FF_CHUNK
    copies = lambda c: _ffn_slab_copies(w1_hbm, w2_hbm, layer, c, stages, sem)

    def tile(load_weights):
        x = x_ref[...]
        xn = _rms(x, g_ref[...]).astype(BF16)
        if load_weights:
            for cp in copies(0):
                cp.start()
        acc = None
        for c in range(n_chunks):
            lo = c * FF_CHUNK
            if load_weights:
                if c + 1 < n_chunks:
                    for cp in copies(c + 1):
                        cp.start()
                for cp in copies(c):
                    cp.wait()
                w1_ref[:, lo:lo + FF_CHUNK] = stage_g[c % 2].astype(BF16)
                w1_ref[:, D_FF + lo:D_FF + lo + FF_CHUNK] = stage_u[c % 2].astype(BF16)
                w2_ref[lo:lo + FF_CHUNK, :] = stage_d[c % 2].astype(BF16)
            g = _dot(xn, w1_ref[:, lo:lo + FF_CHUNK])
            u = _dot(xn, w1_ref[:, D_FF + lo:D_FF + lo + FF_CHUNK])
            h = (jax.nn.silu(g) * u).astype(BF16)
            d = _dot(h, w2_ref[lo:lo + FF_CHUNK, :])
            acc = d if acc is None else acc + d
        y = x + 0.5 * acc
        if final:
            y = _rms(y, fg_ref[...])
        o_ref[...] = y

    first_step = pl.program_id(0) == 0

    @pl.when(first_step)
    def _():
        tile(True)

    @pl.when(jnp.logical_not(first_step))
    def _():
        tile(False)


def _ffn(x, norm_g, w1_stack, w2_stack, layer, final_g=None):
    tok = x.shape[0]
    final = final_g is not None
    hbm = pl.BlockSpec(memory_space=pl.ANY)
    in_specs = [pl.BlockSpec((TM_FFN, D_MODEL), lambda i: (i, 0)), _layer_spec(norm_g, layer), hbm, hbm]
    args = [x, norm_g, w1_stack, w2_stack]
    if final:
        in_specs.append(_layer_spec(final_g, 0))
        args.append(final_g)
    return pl.pallas_call(
        functools.partial(_ffn_kernel, layer=layer, final=final),
        out_shape=jax.ShapeDtypeStruct((tok, D_MODEL), F32),
        grid=(tok // TM_FFN,),
        in_specs=in_specs,
        out_specs=pl.BlockSpec((TM_FFN, D_MODEL), lambda i: (i, 0)),
        scratch_shapes=[pltpu.VMEM((D_MODEL, 2 * D_FF), BF16), pltpu.VMEM((D_FF, D_MODEL), BF16),
                        pltpu.VMEM((2, D_MODEL, FF_CHUNK), F32), pltpu.VMEM((2, D_MODEL, FF_CHUNK), F32),
                        pltpu.VMEM((2, FF_CHUNK, D_MODEL), F32), pltpu.SemaphoreType.DMA((3, 2))],
        compiler_params=_params("arbitrary"),
        name="ffn_final" if final else "ffn",
    )(*args)


def _causal_taps(win, w_ref, n_taps, pad, rows):
    n = win.shape[0]
    offs = [pad - (n_taps - 1) + j for j in range(n_taps)]
    acc = None
    for r in range(SUBLANES):
        taps = [j for j in range(n_taps) if offs[j] % SUBLANES == r]
        if not taps:
            continue
        shifted = win if r == 0 else pltpu.roll(win, n - r, axis=0)
        for j in taps:
            lo = offs[j] - r
            term = w_ref[j:j + 1, :] * shifted[lo:lo + rows, :]
            acc = term if acc is None else acc + term
    return acc


def _linear_scan(a, u, h0):
    rows, width = a.shape
    row = lax.broadcasted_iota(jnp.int32, (SUBLANES, width), 0)
    keeps = [(d, row >= d) for d in (1, 2, 4)]
    h, out = h0, []
    for k in range(rows // SUBLANES):
        av = a[k * SUBLANES:(k + 1) * SUBLANES, :]
        hv = u[k * SUBLANES:(k + 1) * SUBLANES, :]
        for d, keep in keeps:
            hv = jnp.where(keep, av * pltpu.roll(hv, d, axis=0) + hv, hv)
            av = jnp.where(keep, av * pltpu.roll(av, d, axis=0), av)
        hv = hv + av * h
        out.append(hv)
        h = jnp.broadcast_to(hv[SUBLANES - 1:SUBLANES, :], (SUBLANES, width))
    return jnp.concatenate(out, axis=0), h


def _inproj_kernel(x_ref, cos_ref, sin_ref, ng_ref, wa_ref, ba_ref, wlat_ref, blat_ref, wc_ref, bc_ref,
                   qn_ref, wuq_ref, kvn_ref, wukv_ref, cw_ref, cb_ref, wbd_ref, bgate_ref, lam_ref,
                   ya_ref, q_ref, k_ref, v_ref, c_ref, tail_scr, h_scr):
    @pl.when(pl.program_id(1) == 0)
    def _():
        tail_scr[...] = jnp.zeros(tail_scr.shape, F32)
        h_scr[...] = jnp.zeros(h_scr.shape, F32)

    xn = _rms(x_ref[...], ng_ref[...]).astype(BF16)
    pa = _dot(xn, wa_ref[...]) + ba_ref[...]
    xa_pre = pa[:, :LRU_WIDTH]
    win = jnp.concatenate([tail_scr[...], xa_pre], axis=0)
    tail_scr[...] = xa_pre[TM_IN - LRU_PAD:, :]
    xa = cb_ref[...] + _causal_taps(win, cw_ref, LRU_CONV_WIDTH, LRU_PAD, TM_IN)
    gates = _dot(xa.astype(BF16), wbd_ref[...]) + bgate_ref[...]
    r = jax.nn.sigmoid(gates[:, :LRU_WIDTH])
    ig = jax.nn.sigmoid(gates[:, LRU_WIDTH:])
    log_a = (-LRU_C) * r * jax.nn.softplus(-lam_ref[...])
    a = jnp.exp(log_a)
    one_m = (1.0 + a * a) * jnp.tanh(-log_a)
    root = jnp.where(one_m > 0.0, one_m * lax.rsqrt(one_m), 0.0)
    h, h_scr[...] = _linear_scan(a, root * (ig * xa), h_scr[...])
    ya_ref[...] = (h * jax.nn.gelu(pa[:, LRU_WIDTH:])).astype(BF16)
    pc = _dot(xn, wc_ref[...]) + bc_ref[...]
    c_ref[...] = pc[:, :CONV_CH] * jax.nn.sigmoid(pc[:, CONV_CH:])
    lat = _dot(xn, wlat_ref[...]) + blat_ref[...]
    okv, ope = Q_LORA_RANK, Q_LORA_RANK + KV_LORA_RANK
    qq = _dot(_rms(lat[:, :okv], qn_ref[...]).astype(BF16), wuq_ref[...])
    kv = _dot(_rms(lat[:, okv:ope], kvn_ref[...]).astype(BF16), wukv_ref[...])
    cosf, sinf = cos_ref[...], sin_ref[...]
    kpe = lat[:, ope:ope + HEAD_PAD] * cosf + lat[:, ope + HEAD_PAD:] * sinf
    nq = MLA_HEADS * HEAD_PAD
    ones_lane = lax.broadcasted_iota(jnp.int32, (TM_IN, HEAD_PAD), 1) == V_HEAD_DIM
    for hd in range(MLA_HEADS):
        lo = hd * HEAD_PAD
        q_ref[hd] = (qq[:, lo:lo + HEAD_PAD] * cosf + qq[:, nq + lo:nq + lo + HEAD_PAD] * sinf).astype(BF16)
        k_ref[hd] = (kv[:, lo:lo + HEAD_PAD] + kpe).astype(BF16)
        v_ref[hd] = jnp.where(ones_lane, 1.0, kv[:, nq + lo:nq + lo + HEAD_PAD]).astype(BF16)


def _rope_kernel(pos_ref, invf_ref, cos_ref, sin_ref):
    ang = pos_ref[...].astype(F32) * invf_ref[...]
    cos_ref[...] = jnp.cos(ang)
    sin_ref[...] = jnp.sin(ang)


def _rope_tables(positions):
    n_freq = QK_ROPE_DIM // 2
    per_row = LANES // n_freq
    rows = BATCH * SEQ // per_row
    inv_freq = ROPE_THETA ** (-jnp.arange(0, QK_ROPE_DIM, 2, dtype=F32) / QK_ROPE_DIM)
    pos = jnp.repeat(positions.reshape(rows, per_row), n_freq, axis=1)
    invf = jnp.tile(inv_freq, per_row).reshape(1, LANES)
    dense = jax.ShapeDtypeStruct((rows, LANES), F32)
    whole = lambda r: pl.BlockSpec((r, LANES), lambda: (0, 0))
    cos, sin = pl.pallas_call(
        _rope_kernel, out_shape=(dense, dense), in_specs=[whole(rows), whole(1)], out_specs=(whole(rows), whole(rows)),
        name="rope_tables",
    )(pos, invf)

    def on_rope_lanes(t, fill):
        t = t.reshape(BATCH, SEQ, n_freq)
        pad = lambda n: jnp.full((BATCH, SEQ, n), fill, F32)
        return jnp.concatenate([pad(QK_NOPE_DIM), t, t, pad(HEAD_PAD - QK_DIM)], axis=-1)

    return on_rope_lanes(cos, 1.0), on_rope_lanes(sin, 0.0)


def _inproj(x, cos, sin, w, layer):
    nt = SEQ // TM_IN
    tile = lambda n: pl.BlockSpec((None, TM_IN, n), lambda b, i: (b, i, 0))
    heads = lambda n: pl.BlockSpec((None, n, TM_IN, LANES), lambda b, i: (b, 0, i, 0))
    consts = [w["mix_norm"], w["wa"], w["ba"], w["wlat"], w["blat"], w["wc"], w["bc"],
              w["q_norm"], w["wuq"], w["kv_norm"], w["wukv"],
              w["lru_conv_w"], w["lru_conv_b"], w["wbd"], w["bgate"], w["lam"]]
    carry = pltpu.VMEM((LRU_PAD, LRU_WIDTH), F32)
    return pl.pallas_call(
        _inproj_kernel,
        out_shape=(jax.ShapeDtypeStruct((BATCH, SEQ, LRU_WIDTH), BF16),
                   jax.ShapeDtypeStruct((BATCH, MLA_HEADS, SEQ, HEAD_PAD), BF16),
                   jax.ShapeDtypeStruct((BATCH, MLA_HEADS, SEQ, HEAD_PAD), BF16),
                   jax.ShapeDtypeStruct((BATCH, MLA_HEADS, SEQ, HEAD_PAD), BF16),
                   jax.ShapeDtypeStruct((BATCH, SEQ, CONV_CH), F32)),
        grid=(BATCH, nt),
        in_specs=[tile(D_MODEL), tile(HEAD_PAD), tile(HEAD_PAD)] + [_layer_spec(c, layer) for c in consts],
        out_specs=(tile(LRU_WIDTH), heads(MLA_HEADS), heads(MLA_HEADS), heads(MLA_HEADS), tile(CONV_CH)),
        scratch_shapes=[carry, carry],
        compiler_params=_params("parallel", "arbitrary"),
        name="inproj",
    )(x, cos, sin, *consts)


def _attn_kernel(q_ref, k_ref, v_ref, o_ref, s_scr, mx_scr, acc_scr):
    qi = pl.program_id(1)
    groups = TK // LANES
    row = lax.broadcasted_iota(jnp.int32, (TQ, TK), 0)
    col = lax.broadcasted_iota(jnp.int32, (TQ, TK), 1)
    mx_scr[...] = jnp.full(mx_scr.shape, -jnp.inf, F32)

    def scores(j, diagonal):
        k0 = pl.multiple_of(j * TK, TK)
        for h in range(MLA_HEADS):
            s = lax.dot_general(q_ref[h], k_ref[h, pl.ds(k0, TK), :], (((1,), (1,)), ((), ())),
                                preferred_element_type=F32)
            if diagonal:
                s = jnp.where(col <= row, s, -jnp.inf)
            s_scr[h, j] = s
            mx = mx_scr[h]
            for g in range(groups):
                mx = jnp.maximum(mx, s[:, g * LANES:(g + 1) * LANES])
            mx_scr[h] = mx

    def chunkwise(n, fn):
        def four(t, carry):
            for u in range(4):
                fn(4 * t + u)
            return carry

        lax.fori_loop(0, n // 4, four, 0)
        base = (n // 4) * 4

        @pl.when(n % 4 >= 2)
        def _():
            fn(base)
            fn(base + 1)

        @pl.when(n % 2 == 1)
        def _():
            fn(n - 1)

    chunkwise(qi, lambda j: scores(j, False))
    scores(qi, True)

    for h in range(MLA_HEADS):
        mx_scr[h] = jnp.broadcast_to(jnp.max(mx_scr[h], axis=1, keepdims=True), (TQ, LANES))
    acc_scr[...] = jnp.zeros(acc_scr.shape, F32)

    def accumulate(j):
        k0 = pl.multiple_of(j * TK, TK)
        for h in range(MLA_HEADS):
            p = jnp.exp2(s_scr[h, j] - jnp.concatenate([mx_scr[h]] * groups, axis=1))
            acc_scr[h] += _dot(p.astype(BF16), v_ref[h, pl.ds(k0, TK), :])

    chunkwise(qi + 1, accumulate)

    outs = []
    for h in range(MLA_HEADS):
        acc = acc_scr[h]
        outs.append(acc[:, :V_HEAD_DIM] / acc[:, V_HEAD_DIM:V_HEAD_DIM + 1])
    o_ref[...] = jnp.concatenate(outs, axis=1).astype(BF16)


def _attention(q, k, v):
    assert TQ == TK
    stat = pltpu.VMEM((MLA_HEADS, TQ, LANES), F32)
    return pl.pallas_call(
        _attn_kernel,
        out_shape=jax.ShapeDtypeStruct((BATCH, SEQ, MLA_HEADS * V_HEAD_DIM), BF16),
        grid=(BATCH, SEQ // TQ),
        in_specs=[pl.BlockSpec((None, MLA_HEADS, TQ, HEAD_PAD), lambda b, i: (b, 0, i, 0)),
                  pl.BlockSpec((None, MLA_HEADS, SEQ, HEAD_PAD), lambda b, i: (b, 0, 0, 0)),
                  pl.BlockSpec((None, MLA_HEADS, SEQ, HEAD_PAD), lambda b, i: (b, 0, 0, 0))],
        out_specs=pl.BlockSpec((None, TQ, MLA_HEADS * V_HEAD_DIM), lambda b, i: (b, i, 0)),
        scratch_shapes=[pltpu.VMEM((MLA_HEADS, SEQ // TK, TQ, TK), F32), stat, stat],
        compiler_params=_params("parallel", "parallel"),
        name="attention",
    )(q, k, v)


def _outproj_kernel(x_ref, ya_ref, ob_ref, c_ref, halo_ref, ng_ref, wg_ref, bg_ref, dw_ref, db_ref, lng_ref, lnb_ref,
                    wa_ref, wb_ref, wc_ref, bc_ref, wo_ref, o_ref):
    x = x_ref[...]
    xn = _rms(x, ng_ref[...]).astype(BF16)

    def gate(b):
        lo = b * D_MODEL
        return jax.nn.sigmoid(_dot(xn, wg_ref[:, lo:lo + D_MODEL]) + bg_ref[:, lo:lo + D_MODEL])

    merged = gate(0) * _dot(ya_ref[...], wa_ref[...])
    merged = merged + gate(1) * _dot(ob_ref[...], wb_ref[...])
    halo = jnp.where(pl.program_id(1) == 0, 0.0, halo_ref[...])
    win = jnp.concatenate([halo, c_ref[...]], axis=0)
    acc = db_ref[...] + _causal_taps(win, dw_ref, CONV_WIDTH, CONV_PAD, TM_OUT)
    mu = jnp.mean(acc, axis=-1, keepdims=True)
    dlt = acc - mu
    var = jnp.mean(dlt * dlt, axis=-1, keepdims=True)
    yn = dlt * lax.rsqrt(var + NORM_EPS) * lng_ref[...] + lnb_ref[...]
    y_c = _dot(jax.nn.silu(yn).astype(BF16), wc_ref[...]) + bc_ref[...]
    merged = merged + gate(2) * y_c
    o_ref[...] = x + _dot(merged.astype(BF16), wo_ref[...])


def _outproj(x, ya, ob, c, w, layer):
    tile = lambda n: pl.BlockSpec((None, TM_OUT, n), lambda b, i: (b, i, 0))
    halo_blocks = TM_OUT // CONV_PAD
    halo = pl.BlockSpec((None, CONV_PAD, CONV_CH), lambda b, i: (b, jnp.maximum(i * halo_blocks - 1, 0), 0))
    consts = [w["mix_norm"], w["wg"], w["bg"], w["conv_dw_w"], w["conv_dw_b"], w["conv_ln_g"], w["conv_ln_b"],
              w["lru_w_out"], w["mla_w_o"], w["conv_w_out"], w["conv_b_out"], w["w_out"]]
    return pl.pallas_call(
        _outproj_kernel,
        out_shape=jax.ShapeDtypeStruct((BATCH, SEQ, D_MODEL), F32),
        grid=(BATCH, SEQ // TM_OUT),
        in_specs=[tile(D_MODEL), tile(LRU_WIDTH), tile(MLA_HEADS * V_HEAD_DIM), tile(CONV_CH), halo]
        + [_layer_spec(c_, layer) for c_ in consts],
        out_specs=tile(D_MODEL),
        compiler_params=_params("parallel", "parallel"),
        name="outproj",
    )(x, ya, ob, c, c, *consts)


def _prep(p):
    row = lambda v: v.reshape(DEPTH, 1, -1).astype(F32)
    o1, o2, o3 = IN_A, IN_A + IN_B, IN_A + IN_B + IN_C
    w_in, b_in = p["w_in"], p["b_in"]
    oq, okv, ope = o1, o1 + Q_LORA_RANK, o1 + Q_LORA_RANK + KV_LORA_RANK
    half = QK_ROPE_DIM // 2

    def rope_cols(m):
        z = lambda n: jnp.zeros(m.shape[:-1] + (n,), m.dtype)
        plain = jnp.concatenate([z(QK_NOPE_DIM), m, z(HEAD_PAD - QK_DIM)], axis=-1)
        rot = jnp.concatenate([z(QK_NOPE_DIM), -m[..., half:], m[..., :half], z(HEAD_PAD - QK_DIM)], axis=-1)
        return plain, rot

    wpe_plain, wpe_rot = rope_cols(w_in[..., ope:o2])
    bpe_plain, bpe_rot = rope_cols(b_in[..., ope:o2])

    w_uq = p["w_uq"].reshape(DEPTH, Q_LORA_RANK, MLA_HEADS, QK_DIM)
    zq = jnp.zeros((DEPTH, Q_LORA_RANK, MLA_HEADS, HEAD_PAD - QK_DIM), F32)
    q_plain = jnp.concatenate([w_uq, zq], axis=-1)
    q_pe = w_uq[..., QK_NOPE_DIM:]
    q_rot = jnp.concatenate([jnp.zeros((DEPTH, Q_LORA_RANK, MLA_HEADS, QK_NOPE_DIM), F32), -q_pe[..., half:],
                             q_pe[..., :half], zq], axis=-1)
    wuq = jnp.concatenate([q_plain.reshape(DEPTH, Q_LORA_RANK, -1), q_rot.reshape(DEPTH, Q_LORA_RANK, -1)], axis=-1)

    w_ukv = p["w_ukv"].reshape(DEPTH, KV_LORA_RANK, MLA_HEADS, QK_NOPE_DIM + V_HEAD_DIM)
    wk = jnp.concatenate([w_ukv[..., :QK_NOPE_DIM],
                          jnp.zeros((DEPTH, KV_LORA_RANK, MLA_HEADS, HEAD_PAD - QK_NOPE_DIM), F32)], axis=-1)
    wv = jnp.concatenate([w_ukv[..., QK_NOPE_DIM:],
                          jnp.zeros((DEPTH, KV_LORA_RANK, MLA_HEADS, HEAD_PAD - V_HEAD_DIM), F32)], axis=-1)

    wg4 = p["lru_w_gate"]
    eye = jnp.eye(LRU_HEADS, dtype=F32)
    bd = lambda blk: jnp.einsum("lhde,hg->lhdge", blk, eye).reshape(DEPTH, LRU_WIDTH, LRU_WIDTH)
    wbd = jnp.concatenate([bd(wg4[..., :LRU_HEAD_DIM]), bd(wg4[..., LRU_HEAD_DIM:])], axis=-1)
    bgate = jnp.concatenate([p["lru_b_gate"][..., :LRU_HEAD_DIM].reshape(DEPTH, -1),
                             p["lru_b_gate"][..., LRU_HEAD_DIM:].reshape(DEPTH, -1)], axis=-1)

    return dict(
        ffn1_norm=row(p["ffn1_norm"]), ffn2_norm=row(p["ffn2_norm"]), mix_norm=row(p["mix_norm"]),
        wa=w_in[..., :o1].astype(BF16), ba=row(b_in[..., :o1]),
        wlat=jnp.concatenate([w_in[..., oq:ope], wpe_plain, wpe_rot], axis=-1).astype(BF16),
        blat=row(jnp.concatenate([b_in[..., oq:ope], bpe_plain, bpe_rot], axis=-1)),
        wc=w_in[..., o2:o3].astype(BF16), bc=row(b_in[..., o2:o3]),
        wg=w_in[..., o3:].astype(BF16), bg=row(b_in[..., o3:]),
        q_norm=row(p["q_norm"]), wuq=(wuq * (QK_DIM ** -0.5 * LOG2_E)).astype(BF16), kv_norm=row(p["kv_norm"]),
        wukv=jnp.concatenate([wk.reshape(DEPTH, KV_LORA_RANK, -1), wv.reshape(DEPTH, KV_LORA_RANK, -1)],
                             axis=-1).astype(BF16),
        lru_conv_w=p["lru_conv_w"].astype(F32), lru_conv_b=row(p["lru_conv_b"]),
        wbd=wbd.astype(BF16), bgate=row(bgate), lam=row(p["lru_lambda"]),
        conv_dw_w=p["conv_dw_w"].astype(F32), conv_dw_b=row(p["conv_dw_b"]),
        conv_ln_g=row(p["conv_ln_g"]), conv_ln_b=row(p["conv_ln_b"]),
        lru_w_out=p["lru_w_out"].astype(BF16), mla_w_o=p["mla_w_o"].astype(BF16),
        conv_w_out=p["conv_w_out"].astype(BF16), conv_b_out=row(p["conv_b_out"]), w_out=p["w_out"].astype(BF16),
    )


def kernel(x, positions, ffn1_norm, ffn1_w1, ffn1_w2, mix_norm, w_in, b_in, lru_conv_w, lru_conv_b, lru_w_gate, lru_b_gate, lru_lambda, lru_w_out, q_norm, w_uq, kv_norm, w_ukv, mla_w_o, conv_dw_w, conv_dw_b, conv_ln_g, conv_ln_b, conv_w_out, conv_b_out, w_out, ffn2_norm, ffn2_w1, ffn2_w2, final_norm):
    stacked = dict(ffn1_norm=ffn1_norm, mix_norm=mix_norm, w_in=w_in, b_in=b_in,
                   lru_conv_w=lru_conv_w, lru_conv_b=lru_conv_b, lru_w_gate=lru_w_gate, lru_b_gate=lru_b_gate,
                   lru_lambda=lru_lambda, lru_w_out=lru_w_out, q_norm=q_norm, w_uq=w_uq, kv_norm=kv_norm,
                   w_ukv=w_ukv, mla_w_o=mla_w_o, conv_dw_w=conv_dw_w, conv_dw_b=conv_dw_b, conv_ln_g=conv_ln_g,
                   conv_ln_b=conv_ln_b, conv_w_out=conv_w_out, conv_b_out=conv_b_out, w_out=w_out,
                   ffn2_norm=ffn2_norm)
    tok = BATCH * SEQ
    final_g = final_norm.reshape(1, 1, D_MODEL).astype(F32)
    xf = x.reshape(tok, D_MODEL)
    w = _prep(stacked)
    cos, sin = _rope_tables(positions)
    for l in range(DEPTH):
        xf = _ffn(xf, w["ffn1_norm"], ffn1_w1, ffn1_w2, l)
        xb = xf.reshape(BATCH, SEQ, D_MODEL)
        ya, q, k, v, c = _inproj(xb, cos, sin, w, l)
        ob = _attention(q, k, v)
        xf = _outproj(xb, ya, ob, c, w, l).reshape(tok, D_MODEL)
        xf = _ffn(xf, w["ffn2_norm"], ffn2_w1, ffn2_w2, l, final_g if l == DEPTH - 1 else None)
    return xf.reshape(BATCH, SEQ, D_MODEL)
```
